```python
import math
import jax, jax.numpy as jnp
from jax import lax
import numpy as np

D_MODEL = 2048
BATCH = 4
SEQ = 2048
DEPTH = 4
DEC_BATCH = 128
DEC_SEQ = 4
PAST_LEN = 16384
PAGE_SIZE = 128

S5_WIDTH = D_MODEL // 2
CONV_WIDTH = D_MODEL - S5_WIDTH
S5_GROUP_CH = 16
S5_GROUPS = S5_WIDTH // S5_GROUP_CH
S5_STATE = 64
IN_COLS = S5_WIDTH + 2 * CONV_WIDTH
CONV_K = 31
D_FF = (11 * D_MODEL) // 4
FFN_K = 3
PE_DIM = 256
LN_EPS = 1e-5
DT_MIN = 1e-3
DT_MAX = 1e-1
ALPHA = (2.0 * DEPTH) ** 0.25
BETA = (8.0 * DEPTH) ** -0.25

kernel_name = "hymba_s5_conformer_convffn_deepnorm_step"


def _ln(x, g, b):
    xf = x.astype(jnp.float32)
    mu = xf.mean(-1, keepdims=True)
    var = jnp.square(xf - mu).mean(-1, keepdims=True)
    y = (xf - mu) * lax.rsqrt(var + LN_EPS) * g.astype(jnp.float32) + b.astype(jnp.float32)
    return y.astype(x.dtype)


def _causal_dwconv(x, buf, w, b):
    k = w.shape[0]
    c = x.shape[-1]
    xp = jnp.concatenate([buf.astype(x.dtype), x], axis=1)
    y = lax.conv_general_dilated(xp, w.astype(x.dtype)[:, None, :], window_strides=(1,), padding='VALID',
                                 dimension_numbers=('NWC', 'WIO', 'NWC'), feature_group_count=c)
    return y + b.astype(x.dtype), xp[:, xp.shape[1] - (k - 1):]


def _cplx_combine(e1, e2):
    a1r, a1i, b1r, b1i = e1
    a2r, a2i, b2r, b2i = e2
    ar = a1r * a2r - a1i * a2i
    ai = a1r * a2i + a1i * a2r
    br = a2r * b1r - a2i * b1i + b2r
    bi = a2r * b1i + a2i * b1r + b2i
    return (ar, ai, br, bi)


def _s5(u, h0_re, h0_im, lam_re, lam_im, log_dt, b_re, b_im, c_re, c_im, d):
    n, t, _ = u.shape
    f32 = jnp.float32
    uf = u.astype(f32).reshape(n, t, S5_GROUPS, S5_GROUP_CH)
    lr = lam_re.astype(f32)
    li = lam_im.astype(f32)
    dt = jnp.exp(log_dt.astype(f32))[:, None]
    mag = jnp.exp(lr * dt)
    ab_re = mag * jnp.cos(li * dt)
    ab_im = mag * jnp.sin(li * dt)
    num_re = ab_re - 1.0
    num_im = ab_im
    den = lr * lr + li * li
    q_re = (num_re * lr + num_im * li) / den
    q_im = (num_im * lr - num_re * li) / den
    br = b_re.astype(f32)
    bi = b_im.astype(f32)
    bb_re = q_re[..., None] * br - q_im[..., None] * bi
    bb_im = q_re[..., None] * bi + q_im[..., None] * br
    bu_re = jnp.einsum('ntgh,gph->ntgp', uf, bb_re)
    bu_im = jnp.einsum('ntgh,gph->ntgp', uf, bb_im)
    h0r = h0_re.astype(f32)
    h0i = h0_im.astype(f32)
    bu_re = bu_re.at[:, 0].add(ab_re * h0r - ab_im * h0i)
    bu_im = bu_im.at[:, 0].add(ab_re * h0i + ab_im * h0r)
    a_re = jnp.broadcast_to(ab_re, bu_re.shape)
    a_im = jnp.broadcast_to(ab_im, bu_im.shape)
    _, _, h_re, h_im = lax.associative_scan(_cplx_combine, (a_re, a_im, bu_re, bu_im), axis=1)
    y = (jnp.einsum('ghp,ntgp->ntgh', c_re.astype(f32), h_re)
         - jnp.einsum('ghp,ntgp->ntgh', c_im.astype(f32), h_im)
         + d.astype(f32) * uf)
    return y.reshape(n, t, S5_WIDTH), h_re[:, -1], h_im[:, -1]


def _layer(x, p, h0_re, h0_im, conv_buf, ffn_buf, lw):
    (w_in, s5_lam_re, s5_lam_im, s5_log_dt, s5_b_re, s5_b_im, s5_c_re, s5_c_im, s5_d, s5_w_glu,
     conv_w, conv_b, conv_ln_g, conv_ln_b, w_out, ln1_g, ln1_b,
     ffn_w_up, ffn_conv_w, ffn_conv_b, ffn_w_down, ln2_g, ln2_b,
     pe_w, pe_w_gate, ln3_g, ln3_b) = lw
    z = x @ w_in
    u = z[..., :S5_WIDTH]
    cv = z[..., S5_WIDTH:S5_WIDTH + CONV_WIDTH]
    cg = z[..., S5_WIDTH + CONV_WIDTH:]
    y5, h_re, h_im = _s5(u, h0_re, h0_im, s5_lam_re, s5_lam_im, s5_log_dt,
                         s5_b_re, s5_b_im, s5_c_re, s5_c_im, s5_d)
    g5 = jax.nn.gelu(y5.astype(x.dtype))
    s5_out = g5 * jax.nn.sigmoid(g5 @ s5_w_glu)
    c = cv * jax.nn.sigmoid(cg)
    c, new_conv = _causal_dwconv(c, conv_buf, conv_w, conv_b)
    c = jax.nn.silu(_ln(c, conv_ln_g, conv_ln_b))
    mix = jnp.concatenate([s5_out, c], axis=-1) @ w_out
    x = _ln(ALPHA * x + mix, ln1_g, ln1_b)
    hup = x @ ffn_w_up
    hup, new_ffn = _causal_dwconv(hup, ffn_buf, ffn_conv_w, ffn_conv_b)
    gate, val = jnp.split(hup, 2, axis=-1)
    x = _ln(ALPHA * x + (jax.nn.silu(gate) * val) @ ffn_w_down, ln2_g, ln2_b)
    e = p @ pe_w
    x = _ln(ALPHA * x + jax.nn.sigmoid(x @ pe_w_gate) * e, ln3_g, ln3_b)
    return x, h_re, h_im, new_conv, new_ffn


def setup_inputs(seed: int = 0) -> dict:
    key = jax.random.key(seed)
    ks = jax.random.split(key, 48)
    f32 = jnp.float32
    nrm = lambda k, s: jax.random.normal(k, s, f32)
    L = DEPTH
    n_idx = jnp.arange(S5_STATE, dtype=f32)
    inp = {}
    inp['x_prompt'] = nrm(ks[0], (BATCH, SEQ, D_MODEL))
    inp['x_sample'] = nrm(ks[1], (DEC_BATCH, DEC_SEQ, D_MODEL))
    inp['state_s5_re'] = 0.3 * nrm(ks[2], (L, DEC_BATCH, S5_GROUPS, S5_STATE))
    inp['state_s5_im'] = 0.3 * nrm(ks[3], (L, DEC_BATCH, S5_GROUPS, S5_STATE))
    inp['cache_conv'] = 0.5 * nrm(ks[4], (L, DEC_BATCH, CONV_K - 1, CONV_WIDTH))
    inp['cache_ffn_conv'] = nrm(ks[5], (L, DEC_BATCH, FFN_K - 1, 2 * D_FF))
    inp['p_prompt'] = nrm(ks[6], (L, BATCH, SEQ, PE_DIM))
    inp['p_sample'] = nrm(ks[7], (L, DEC_BATCH, DEC_SEQ, PE_DIM))
    inp['w_in'] = nrm(ks[8], (L, D_MODEL, IN_COLS)) * D_MODEL ** -0.5
    inp['s5_lam_re'] = -0.5 + 0.01 * nrm(ks[9], (L, S5_GROUPS, S5_STATE))
    inp['s5_lam_im'] = math.pi * n_idx + 0.01 * nrm(ks[10], (L, S5_GROUPS, S5_STATE))
    inp['s5_log_dt'] = jax.random.uniform(ks[11], (L, S5_GROUPS), f32, math.log(DT_MIN), math.log(DT_MAX))
    inp['s5_b_re'] = nrm(ks[12], (L, S5_GROUPS, S5_STATE, S5_GROUP_CH)) * (2 * S5_GROUP_CH) ** -0.5
    inp['s5_b_im'] = nrm(ks[13], (L, S5_GROUPS, S5_STATE, S5_GROUP_CH)) * (2 * S5_GROUP_CH) ** -0.5
    inp['s5_c_re'] = nrm(ks[14], (L, S5_GROUPS, S5_GROUP_CH, S5_STATE)) * (2 * S5_STATE) ** -0.5
    inp['s5_c_im'] = nrm(ks[15], (L, S5_GROUPS, S5_GROUP_CH, S5_STATE)) * (2 * S5_STATE) ** -0.5
    inp['s5_d'] = nrm(ks[16], (L, S5_GROUPS, S5_GROUP_CH))
    inp['s5_w_glu'] = nrm(ks[17], (L, S5_WIDTH, S5_WIDTH)) * S5_WIDTH ** -0.5
    inp['conv_w'] = nrm(ks[18], (L, CONV_K, CONV_WIDTH)) * CONV_K ** -0.5
    inp['conv_b'] = 0.01 * nrm(ks[19], (L, CONV_WIDTH))
    inp['conv_ln_g'] = 1.0 + 0.02 * nrm(ks[20], (L, CONV_WIDTH))
    inp['conv_ln_b'] = 0.01 * nrm(ks[21], (L, CONV_WIDTH))
    inp['w_out'] = nrm(ks[22], (L, D_MODEL, D_MODEL)) * D_MODEL ** -0.5 * BETA
    inp['ln1_g'] = 1.0 + 0.02 * nrm(ks[23], (L, D_MODEL))
    inp['ln1_b'] = 0.01 * nrm(ks[24], (L, D_MODEL))
    inp['ffn_w_up'] = nrm(ks[25], (L, D_MODEL, 2 * D_FF)) * D_MODEL ** -0.5
    inp['ffn_conv_w'] = nrm(ks[26], (L, FFN_K, 2 * D_FF)) * FFN_K ** -0.5
    inp['ffn_conv_b'] = 0.01 * nrm(ks[27], (L, 2 * D_FF))
    inp['ffn_w_down'] = nrm(ks[28], (L, D_FF, D_MODEL)) * D_FF ** -0.5 * BETA
    inp['ln2_g'] = 1.0 + 0.02 * nrm(ks[29], (L, D_MODEL))
    inp['ln2_b'] = 0.01 * nrm(ks[30], (L, D_MODEL))
    inp['pe_w'] = nrm(ks[31], (L, PE_DIM, D_MODEL)) * PE_DIM ** -0.5 * BETA
    inp['pe_w_gate'] = nrm(ks[32], (L, D_MODEL, D_MODEL)) * D_MODEL ** -0.5
    inp['ln3_g'] = 1.0 + 0.02 * nrm(ks[33], (L, D_MODEL))
    inp['ln3_b'] = 0.01 * nrm(ks[34], (L, D_MODEL))
    return inp


def reference(x_prompt, x_sample, state_s5_re, state_s5_im, cache_conv, cache_ffn_conv, p_prompt, p_sample,
              w_in, s5_lam_re, s5_lam_im, s5_log_dt, s5_b_re, s5_b_im, s5_c_re, s5_c_im, s5_d, s5_w_glu,
              conv_w, conv_b, conv_ln_g, conv_ln_b, w_out, ln1_g, ln1_b,
              ffn_w_up, ffn_conv_w, ffn_conv_b, ffn_w_down, ln2_g, ln2_b,
              pe_w, pe_w_gate, ln3_g, ln3_b):
    nb = x_prompt.shape[0]
    xp = x_prompt
    xs = x_sample
    zero_h = jnp.zeros((nb, S5_GROUPS, S5_STATE), jnp.float32)
    zero_conv = jnp.zeros((nb, CONV_K - 1, CONV_WIDTH), x_prompt.dtype)
    zero_ffn = jnp.zeros((nb, FFN_K - 1, 2 * D_FF), x_prompt.dtype)
    p_re, p_im, p_cv, p_ff = [], [], [], []
    s_re, s_im, s_cv, s_ff = [], [], [], []
    for i in range(DEPTH):
        lw = (w_in[i], s5_lam_re[i], s5_lam_im[i], s5_log_dt[i], s5_b_re[i], s5_b_im[i], s5_c_re[i], s5_c_im[i],
              s5_d[i], s5_w_glu[i], conv_w[i], conv_b[i], conv_ln_g[i], conv_ln_b[i], w_out[i], ln1_g[i], ln1_b[i],
              ffn_w_up[i], ffn_conv_w[i], ffn_conv_b[i], ffn_w_down[i], ln2_g[i], ln2_b[i],
              pe_w[i], pe_w_gate[i], ln3_g[i], ln3_b[i])
        xp, hr, hi, cvb, ffb = _layer(xp, p_prompt[i], zero_h, zero_h, zero_conv, zero_ffn, lw)
        p_re.append(hr); p_im.append(hi); p_cv.append(cvb); p_ff.append(ffb)
        xs, hr, hi, cvb, ffb = _layer(xs, p_sample[i], state_s5_re[i], state_s5_im[i],
                                      cache_conv[i], cache_ffn_conv[i], lw)
        s_re.append(hr); s_im.append(hi); s_cv.append(cvb); s_ff.append(ffb)
    return (xp, xs,
            jnp.stack(p_re), jnp.stack(p_im), jnp.stack(p_cv), jnp.stack(p_ff),
            jnp.stack(s_re), jnp.stack(s_im), jnp.stack(s_cv), jnp.stack(s_ff))
```

```python
import functools
import math

import jax
import jax.numpy as jnp
from jax import lax
from jax.experimental import pallas as pl
from jax.experimental.pallas import tpu as pltpu

F32 = jnp.float32
BF16 = jnp.bfloat16

LN_EPS = 1e-5
V7X_VMEM_LIMIT_BYTES = 56 * 1024 * 1024
SUBLANES = 8
LANES = 128
S5_GROUP_CH = 16
S5_STATE = 64
SUPER = 4


def _params(sem):
    return pltpu.CompilerParams(dimension_semantics=sem, vmem_limit_bytes=V7X_VMEM_LIMIT_BYTES)


def _resident(shape):
    nd = len(shape)
    return pl.BlockSpec(shape, lambda *_: (0,) * nd, pipeline_mode=pl.Buffered(1))


def _ln(r, g, b):
    mu = jnp.mean(r, axis=-1, keepdims=True)
    d = r - mu
    var = jnp.mean(d * d, axis=-1, keepdims=True)
    return d * lax.rsqrt(var + LN_EPS) * g + b


def _in_proj_kernel(x_ref, w_ref, u_ref, c_ref, *, s5w, cw):
    xb = x_ref[...].astype(BF16)
    u_ref[...] = jnp.dot(xb, w_ref[:, :s5w], preferred_element_type=F32)
    cv = jnp.dot(xb, w_ref[:, s5w:s5w + cw], preferred_element_type=F32)
    cg = jnp.dot(xb, w_ref[:, s5w + cw:], preferred_element_type=F32)
    c_ref[...] = cv * jax.nn.sigmoid(cg)


def _in_proj(x, w_in_bf, s5w, cw, tm):
    n, d = x.shape
    return pl.pallas_call(
        functools.partial(_in_proj_kernel, s5w=s5w, cw=cw),
        grid=(n // tm,),
        in_specs=[pl.BlockSpec((tm, d), lambda i: (i, 0)), _resident(w_in_bf.shape)],
        out_specs=[pl.BlockSpec((tm, s5w), lambda i: (i, 0)), pl.BlockSpec((tm, cw), lambda i: (i, 0))],
        out_shape=[jax.ShapeDtypeStruct((n, s5w), F32), jax.ShapeDtypeStruct((n, cw), F32)],
        compiler_params=_params(("arbitrary",)),
        name="in_proj",
    )(x, w_in_bf)


def _s5_seq_kernel(u_ref, wb_ref, wc_ref, are_ref, aim_ref, d_ref, h0re_ref, h0im_ref,
                   y_ref, hre_ref, him_ref, sre, sim, *, tc, sgw):
    step_rows = SUBLANES
    k = pl.program_id(1)

    @pl.when(k == 0)
    def _():
        hre_ref[...] = h0re_ref[...]
        him_ref[...] = h0im_ref[...]

    u = u_ref[...]
    ub = u.astype(BF16)
    for v in range(SUPER):
        r = jnp.dot(ub[:, v * sgw:(v + 1) * sgw], wb_ref[v], preferred_element_type=F32)
        for i in range(SUBLANES):
            sre[v, pl.ds(i, tc, stride=step_rows), :] = r[:, i * LANES:(i + 1) * LANES]
            sim[v, pl.ds(i, tc, stride=step_rows), :] = r[:, (SUBLANES + i) * LANES:(SUBLANES + i + 1) * LANES]

    a_re = [are_ref[v * SUBLANES:(v + 1) * SUBLANES, :] for v in range(SUPER)]
    a_im = [aim_ref[v * SUBLANES:(v + 1) * SUBLANES, :] for v in range(SUPER)]
    init = tuple(hre_ref[0, v * SUBLANES:(v + 1) * SUBLANES, :] for v in range(SUPER)) + \
        tuple(him_ref[0, v * SUBLANES:(v + 1) * SUBLANES, :] for v in range(SUPER))

    def step(t, carry):
        row = pl.multiple_of(t * step_rows, step_rows)
        new_re, new_im = [], []
        for v in range(SUPER):
            hr, hi = carry[v], carry[SUPER + v]
            nr = a_re[v] * hr - a_im[v] * hi + sre[v, pl.ds(row, step_rows), :]
            ni = a_re[v] * hi + a_im[v] * hr + sim[v, pl.ds(row, step_rows), :]
            sre[v, pl.ds(row, step_rows), :] = nr
            sim[v, pl.ds(row, step_rows), :] = ni
            new_re.append(nr)
            new_im.append(ni)
        return tuple(new_re) + tuple(new_im)

    fin = lax.fori_loop(0, tc, step, init, unroll=8)
    for v in range(SUPER):
        hre_ref[0, v * SUBLANES:(v + 1) * SUBLANES, :] = fin[v]
        him_ref[0, v * SUBLANES:(v + 1) * SUBLANES, :] = fin[SUPER + v]

    for v in range(SUPER):
        cols = [sre[v, pl.ds(i, tc, stride=step_rows), :] for i in range(SUBLANES)]
        cols += [sim[v, pl.ds(i, tc, stride=step_rows), :] for i in range(SUBLANES)]
        hcat = jnp.concatenate(cols, axis=1).astype(BF16)
        y = jnp.dot(hcat, wc_ref[v], preferred_element_type=F32)
        y_ref[:, v * sgw:(v + 1) * sgw] = y + d_ref[:, v * sgw:(v + 1) * sgw] * u[:, v * sgw:(v + 1) * sgw]


def _s5_seq(u, wb, wc, a_re, a_im, d, h0re, h0im, nb, t, tc):
    s5w = u.shape[1]
    sgw = s5w // SUPER
    nk = t // tc
    nstate = a_re.shape[0] * a_re.shape[1]
    rows = nstate // LANES
    st_spec = pl.BlockSpec((1, rows, LANES), lambda b, k: (b, 0, 0))
    return pl.pallas_call(
        functools.partial(_s5_seq_kernel, tc=tc, sgw=sgw),
        grid=(nb, nk),
        in_specs=[pl.BlockSpec((tc, s5w), lambda b, k: (b * nk + k, 0)),
                  _resident(wb.shape), _resident(wc.shape), _resident(a_re.shape), _resident(a_im.shape),
                  _resident(d.shape), st_spec, st_spec],
        out_specs=[pl.BlockSpec((tc, s5w), lambda b, k: (b * nk + k, 0)), st_spec, st_spec],
        out_shape=[jax.ShapeDtypeStruct((nb * t, s5w), F32),
                   jax.ShapeDtypeStruct((nb, rows, LANES), F32), jax.ShapeDtypeStruct((nb, rows, LANES), F32)],
        scratch_shapes=[pltpu.VMEM((SUPER, tc * SUBLANES, LANES), F32),
                        pltpu.VMEM((SUPER, tc * SUBLANES, LANES), F32)],
        compiler_params=_params(("arbitrary", "arbitrary")),
        name="s5_seq",
    )(u, wb, wc, a_re, a_im, d, h0re, h0im)


def _s5_planes_kernel(u_ref, wb_ref, wc_ref, are_ref, aim_ref, d_ref, h0re_ref, h0im_ref,
                      y_ref, hre_ref, him_ref, *, nt, nseq, half):
    u = u_ref[...]
    r = jnp.dot(u.astype(BF16), wb_ref[0], preferred_element_type=F32)
    a_re, a_im = are_ref[...], aim_ref[...]
    hr, hi = h0re_ref[...], h0im_ref[...]
    hs_re, hs_im = [], []
    for t in range(nt):
        br = r[t * nseq:(t + 1) * nseq, :half]
        bi = r[t * nseq:(t + 1) * nseq, half:]
        hr, hi = a_re * hr - a_im * hi + br, a_re * hi + a_im * hr + bi
        hs_re.append(hr)
        hs_im.append(hi)
    hre_ref[...] = hr
    him_ref[...] = hi
    hcat = jnp.concatenate([jnp.concatenate(hs_re, axis=0), jnp.concatenate(hs_im, axis=0)], axis=1)
    y = jnp.dot(hcat.astype(BF16), wc_ref[0], preferred_element_type=F32)
    y_ref[...] = y + d_ref[...] * u


def _s5_planes(u, wb, wc, a_re, a_im, d, h0re, h0im, nt, nseq):
    n, s5w = u.shape
    sgw = s5w // SUPER
    nstate = h0re.shape[1]
    half = nstate // SUPER
    return pl.pallas_call(
        functools.partial(_s5_planes_kernel, nt=nt, nseq=nseq, half=half),
        grid=(SUPER,),
        in_specs=[pl.BlockSpec((n, sgw), lambda v: (0, v)),
                  pl.BlockSpec((1,) + wb.shape[1:], lambda v: (v, 0, 0)),
                  pl.BlockSpec((1,) + wc.shape[1:], lambda v: (v, 0, 0)),
                  pl.BlockSpec((1, half), lambda v: (0, v)), pl.BlockSpec((1, half), lambda v: (0, v)),
                  pl.BlockSpec((1, sgw), lambda v: (0, v)),
                  pl.BlockSpec((nseq, half), lambda v: (0, v)), pl.BlockSpec((nseq, half), lambda v: (0, v))],
        out_specs=[pl.BlockSpec((n, sgw), lambda v: (0, v)),
                   pl.BlockSpec((nseq, half), lambda v: (0, v)), pl.BlockSpec((nseq, half), lambda v: (0, v))],
        out_shape=[jax.ShapeDtypeStruct((n, s5w), F32),
                   jax.ShapeDtypeStruct((nseq, nstate), F32), jax.ShapeDtypeStruct((nseq, nstate), F32)],
        compiler_params=_params(("arbitrary",)),
        name="s5_planes",
    )(u, wb, wc, a_re, a_im, d, h0re, h0im)


CONV_ROW_TILE = 32


def _conv_seq_kernel(c_ref, halo_ref, w_ref, b_ref, g_ref, bt_ref, out_ref, xp_ref, sh_ref,
                     *, tm, kk, halo, blocks_per_seq):
    i = pl.program_id(0)
    start = (i % blocks_per_seq) == 0
    xp_ref[0:halo, :] = jnp.where(start, 0.0, halo_ref[...])
    xp_ref[halo:halo + tm, :] = c_ref[...]
    ncopy = tm + halo - SUBLANES
    for b in range(1, SUBLANES):
        sh_ref[b - 1, :, :] = xp_ref[pl.ds(b, ncopy), :]
    rt = CONV_ROW_TILE
    bias = b_ref[...]
    gam, bet = g_ref[...], bt_ref[...]

    def body(r, carry):
        r0 = pl.multiple_of(r * rt, rt)
        acc = jnp.broadcast_to(bias, (rt, bias.shape[1]))
        for j in range(kk):
            a, b = divmod(halo - j, SUBLANES)
            k = kk - 1 - j
            if b == 0:
                xs = xp_ref[pl.ds(r0 + SUBLANES * a, rt), :]
            else:
                xs = sh_ref[b - 1, pl.ds(r0 + SUBLANES * a, rt), :]
            w8 = w_ref[SUBLANES * k:SUBLANES * (k + 1), :]
            acc = acc + jnp.concatenate([w8] * (rt // SUBLANES), axis=0) * xs
        out_ref[pl.ds(r0, rt), :] = jax.nn.silu(_ln(acc, gam, bet))
        return carry

    lax.fori_loop(0, tm // rt, body, 0)


def _conv_seq(c, w8, b, g, bt, t, tm, kk):
    n, cw = c.shape
    halo = 32
    assert kk - 1 <= halo and tm % halo == 0 and t % tm == 0
    hb = tm // halo
    return pl.pallas_call(
        functools.partial(_conv_seq_kernel, tm=tm, kk=kk, halo=halo, blocks_per_seq=t // tm),
        grid=(n // tm,),
        in_specs=[pl.BlockSpec((tm, cw), lambda i: (i, 0)),
                  pl.BlockSpec((halo, cw), lambda i: (jnp.maximum(i * hb - 1, 0), 0)),
                  _resident(w8.shape), _resident(b.shape), _resident(g.shape), _resident(bt.shape)],
        out_specs=pl.BlockSpec((tm, cw), lambda i: (i, 0)),
        out_shape=jax.ShapeDtypeStruct((n, cw), F32),
        scratch_shapes=[pltpu.VMEM((tm + halo, cw), F32),
                        pltpu.VMEM((SUBLANES - 1, tm + halo - SUBLANES, cw), F32)],
        compiler_params=_params(("arbitrary",)),
        name="conv_seq",
    )(c, c, w8, b, g, bt)


def _conv_planes_kernel(cache_ref, c_ref, w_ref, b_ref, g_ref, bt_ref, out_ref, *, nt, kk):
    rows, cw = c_ref.shape[1], c_ref.shape[2]
    gam, bet = g_ref[...], bt_ref[...]
    for t in range(nt):
        acc = jnp.broadcast_to(b_ref[...], (rows, cw))
        for k in range(kk):
            p = t + k
            xs = cache_ref[p] if p < kk - 1 else c_ref[p - (kk - 1)]
            w8 = w_ref[SUBLANES * k:SUBLANES * (k + 1), :]
            acc = acc + jnp.concatenate([w8] * (rows // SUBLANES), axis=0) * xs
        out_ref[t] = jax.nn.silu(_ln(acc, gam, bet))


def _conv_planes(cache_t, c_t, w8, b, g, bt, kk, rows):
    nt, nseq, cw = c_t.shape
    return pl.pallas_call(
        functools.partial(_conv_planes_kernel, nt=nt, kk=kk),
        grid=(nseq // rows,),
        in_specs=[pl.BlockSpec((kk - 1, rows, cw), lambda i: (0, i, 0)),
                  pl.BlockSpec((nt, rows, cw), lambda i: (0, i, 0)),
                  _resident(w8.shape), _resident(b.shape), _resident(g.shape), _resident(bt.shape)],
        out_specs=pl.BlockSpec((nt, rows, cw), lambda i: (0, i, 0)),
        out_shape=jax.ShapeDtypeStruct((nt, nseq, cw), F32),
        compiler_params=_params(("arbitrary",)),
        name="conv_planes",
    )(cache_t, c_t, w8, b, g, bt)


def _mix_out_kernel(y5_ref, ca_ref, x_ref, wglu_ref, wout_ref, g_ref, b_ref, out_ref, *, alpha, s5w):
    g5 = jax.nn.gelu(y5_ref[...])
    gate = jnp.dot(g5.astype(BF16), wglu_ref[...], preferred_element_type=F32)
    s5o = g5 * jax.nn.sigmoid(gate)
    mix = jnp.dot(s5o.astype(BF16), wout_ref[:s5w, :], preferred_element_type=F32)
    mix = mix + jnp.dot(ca_ref[...].astype(BF16), wout_ref[s5w:, :], preferred_element_type=F32)
    out_ref[...] = _ln(alpha * x_ref[...] + mix, g_ref[...], b_ref[...])


def _mix_out(y5, ca, x, wglu, wout, g, b, alpha, tm):
    n, d = x.shape
    s5w, cw = y5.shape[1], ca.shape[1]
    return pl.pallas_call(
        functools.partial(_mix_out_kernel, alpha=alpha, s5w=s5w),
        grid=(n // tm,),
        in_specs=[pl.BlockSpec((tm, s5w), lambda i: (i, 0)), pl.BlockSpec((tm, cw), lambda i: (i, 0)),
                  pl.BlockSpec((tm, d), lambda i: (i, 0)),
                  _resident(wglu.shape), _resident(wout.shape), _resident(g.shape), _resident(b.shape)],
        out_specs=pl.BlockSpec((tm, d), lambda i: (i, 0)),
        out_shape=jax.ShapeDtypeStruct((n, d), F32),
        compiler_params=_params(("arbitrary",)),
        name="mix_out",
    )(y5, ca, x, wglu, wout, g, b)


FFN_HALO = 16


def _ffn_kernel(*refs, alpha, tm, planes, nseq, blocks_per_seq):
    if planes:
        (x_ref, hg_c_ref, hv_c_ref, wg_ref, wv_ref, wd_ref, cwg_ref, cwv_ref, cbg_ref, cbv_ref, g_ref, b_ref,
         out_ref, tg_ref, tv_ref, xb_ref, acc_ref) = refs
    else:
        (x_ref, halo_ref, wg_ref, wv_ref, wd_ref, cwg_ref, cwv_ref, cbg_ref, cbv_ref, g_ref, b_ref,
         out_ref, tg_ref, tv_ref, xb_ref, acc_ref) = refs
    i = pl.program_id(0)
    j = pl.program_id(1)

    @pl.when(j == 0)
    def _():
        if planes:
            xb_ref[...] = x_ref[...].astype(BF16)
        else:
            start = (i % blocks_per_seq) == 0
            xb_ref[0:FFN_HALO, :] = jnp.where(start, 0.0, halo_ref[...]).astype(BF16)
            xb_ref[FFN_HALO:, :] = x_ref[...].astype(BF16)
        acc_ref[...] = jnp.zeros_like(acc_ref)

    xb = xb_ref[...]
    hg = jnp.dot(xb, wg_ref[...], preferred_element_type=F32)
    hv = jnp.dot(xb, wv_ref[...], preferred_element_type=F32)

    def conv3(h, hist_ref, cw_ref, cb_ref, tail_ref):
        w0, w1, w2 = cw_ref[0:1, :], cw_ref[1:2, :], cw_ref[2:3, :]
        if planes:
            hp = jnp.concatenate([hist_ref[...], h], axis=0)
            cur, prev1, prev2 = hp[2 * nseq:], hp[nseq:nseq + tm], hp[:tm]
            tail_ref[...] = h[tm - 2 * nseq:, :]
        else:
            cur = h[FFN_HALO:, :]
            prev1 = h[FFN_HALO - 1:FFN_HALO - 1 + tm, :]
            prev2 = h[FFN_HALO - 2:FFN_HALO - 2 + tm, :]
            tail_ref[...] = h[FFN_HALO + tm - SUBLANES:, :]
        return w2 * cur + w1 * prev1 + w0 * prev2 + cb_ref[...]

    cg = conv3(hg, hg_c_ref if planes else None, cwg_ref, cbg_ref, tg_ref)
    cv = conv3(hv, hv_c_ref if planes else None, cwv_ref, cbv_ref, tv_ref)
    act = (jax.nn.silu(cg) * cv).astype(BF16)
    acc_ref[...] += jnp.dot(act, wd_ref[...], preferred_element_type=F32)

    @pl.when(j == pl.num_programs(1) - 1)
    def _():
        out_ref[...] = _ln(alpha * x_ref[...] + acc_ref[...], g_ref[...], b_ref[...])


def _ffn(x, w_up, w_down, conv_w, conv_b, g, b, alpha, tm, tf, *, cache=None, t=None, nseq=None):
    n, d = x.shape
    dff = w_down.shape[0]
    nj = dff // tf
    planes = cache is not None
    wspecs = [pl.BlockSpec((d, tf), lambda i, j: (0, j)), pl.BlockSpec((d, tf), lambda i, j: (0, nj + j)),
              pl.BlockSpec((tf, d), lambda i, j: (j, 0)),
              pl.BlockSpec((conv_w.shape[0], tf), lambda i, j: (0, j)),
              pl.BlockSpec((conv_w.shape[0], tf), lambda i, j: (0, nj + j)),
              pl.BlockSpec((1, tf), lambda i, j: (0, j)), pl.BlockSpec((1, tf), lambda i, j: (0, nj + j)),
              pl.BlockSpec((1, d), lambda i, j: (0, 0)), pl.BlockSpec((1, d), lambda i, j: (0, 0))]
    wargs = [w_up, w_up, w_down, conv_w, conv_w, conv_b, conv_b, g, b]
    if planes:
        assert n == tm
        hist = cache.shape[0]
        in_specs = [pl.BlockSpec((tm, d), lambda i, j: (i, 0)),
                    pl.BlockSpec((hist, tf), lambda i, j: (0, j)),
                    pl.BlockSpec((hist, tf), lambda i, j: (0, nj + j))] + wspecs
        args = [x, cache, cache] + wargs
        tail_rows, xb_rows, bps = hist, tm, 1
    else:
        hb = tm // FFN_HALO
        in_specs = [pl.BlockSpec((tm, d), lambda i, j: (i, 0)),
                    pl.BlockSpec((FFN_HALO, d), lambda i, j: (jnp.maximum(i * hb - 1, 0), 0))] + wspecs
        args = [x, x] + wargs
        tail_rows, xb_rows, bps = SUBLANES, tm + FFN_HALO, t // tm
    nblk = n // tm
    tail_spec = pl.BlockSpec((tail_rows, tf), lambda i, j: (i, j))
    return pl.pallas_call(
        functools.partial(_ffn_kernel, alpha=alpha, tm=tm, planes=planes, nseq=nseq, blocks_per_seq=bps),
        grid=(nblk, nj),
        in_specs=in_specs,
        out_specs=[pl.BlockSpec((tm, d), lambda i, j: (i, 0)), tail_spec, tail_spec],
        out_shape=[jax.ShapeDtypeStruct((n, d), F32),
                   jax.ShapeDtypeStruct((nblk * tail_rows, dff), F32),
                   jax.ShapeDtypeStruct((nblk * tail_rows, dff), F32)],
        scratch_shapes=[pltpu.VMEM((xb_rows, d), BF16), pltpu.VMEM((tm, d), F32)],
        compiler_params=_params(("arbitrary", "arbitrary")),
        name="ffn_planes" if planes else "ffn_seq",
    )(*args)


def _pe_kernel(x_ref, p_ref, wg_ref, wp_ref, g_ref, b_ref, out_ref, *, alpha):
    x = x_ref[...]
    gate = jax.nn.sigmoid(jnp.dot(x.astype(BF16), wg_ref[...], preferred_element_type=F32))
    e = jnp.dot(p_ref[...].astype(BF16), wp_ref[...], preferred_element_type=F32)
    out_ref[...] = _ln(alpha * x + gate * e, g_ref[...], b_ref[...])


def _pe(x, p, wg, wp, g, b, alpha, tm):
    n, d = x.shape
    pd = p.shape[1]
    return pl.pallas_call(
        functools.partial(_pe_kernel, alpha=alpha),
        grid=(n // tm,),
        in_specs=[pl.BlockSpec((tm, d), lambda i: (i, 0)), pl.BlockSpec((tm, pd), lambda i: (i, 0)),
                  _resident(wg.shape), _resident(wp.shape), _resident(g.shape), _resident(b.shape)],
        out_specs=pl.BlockSpec((tm, d), lambda i: (i, 0)),
        out_shape=jax.ShapeDtypeStruct((n, d), F32),
        compiler_params=_params(("arbitrary",)),
        name="pe_embed",
    )(x, p, wg, wp, g, b)


def _s5_discretise(lam_re, lam_im, log_dt, b_re, b_im):
    lr, li = lam_re.astype(F32), lam_im.astype(F32)
    dt = jnp.exp(log_dt.astype(F32))[:, None]
    mag = jnp.exp(lr * dt)
    ab_re = mag * jnp.cos(li * dt)
    ab_im = mag * jnp.sin(li * dt)
    num_re, num_im = ab_re - 1.0, ab_im
    den = lr * lr + li * li
    q_re = (num_re * lr + num_im * li) / den
    q_im = (num_im * lr - num_re * li) / den
    br, bi = b_re.astype(F32), b_im.astype(F32)
    bb_re = q_re[..., None] * br - q_im[..., None] * bi
    bb_im = q_re[..., None] * bi + q_im[..., None] * br
    return ab_re, ab_im, bb_re, bb_im


def _s5_block_weights(bb_re, bb_im, c_re, c_im):
    g, p, h = bb_re.shape
    gs = g // SUPER
    eye = jnp.eye(gs, dtype=F32)

    def b_blocks(bb):
        t = bb.reshape(SUPER, gs, p, h).transpose(0, 1, 3, 2)
        return jnp.einsum('sghp,gk->sghkp', t, eye).reshape(SUPER, gs * h, gs * p)

    def c_blocks(c):
        t = c.astype(F32).reshape(SUPER, gs, h, p)
        return jnp.einsum('sghp,gk->sgpkh', t, eye).reshape(SUPER, gs * p, gs * h)

    wb = jnp.concatenate([b_blocks(bb_re), b_blocks(bb_im)], axis=2).astype(BF16)
    wc = jnp.concatenate([c_blocks(c_re), -c_blocks(c_im)], axis=1).astype(BF16)
    return wb, wc


def _row(v):
    return v.astype(F32).reshape(1, -1)


def kernel(x_prompt, x_sample, state_s5_re, state_s5_im, cache_conv, cache_ffn_conv, p_prompt, p_sample,
           w_in, s5_lam_re, s5_lam_im, s5_log_dt, s5_b_re, s5_b_im, s5_c_re, s5_c_im, s5_d, s5_w_glu,
           conv_w, conv_b, conv_ln_g, conv_ln_b, w_out, ln1_g, ln1_b,
           ffn_w_up, ffn_conv_w, ffn_conv_b, ffn_w_down, ln2_g, ln2_b,
           pe_w, pe_w_gate, ln3_g, ln3_b):
    nb, t, d = x_prompt.shape
    ns, ts, _ = x_sample.shape
    depth = w_in.shape[0]
    groups, nstate_g = s5_lam_re.shape[1], s5_lam_re.shape[2]
    s5w = groups * s5_d.shape[2]
    cw = d - s5w
    nstate = groups * nstate_g
    kk = conv_w.shape[1]
    fk = ffn_conv_w.shape[1]
    dff = ffn_w_down.shape[1]
    assert fk == 3 and ts >= fk - 1 and ts <= kk - 1 and t >= kk - 1
    alpha = (2.0 * depth) ** 0.25
    tm, tf, tc, tconv = 512, 512, 256, 256

    xp = x_prompt.reshape(nb * t, d)
    xs = x_sample.transpose(1, 0, 2).reshape(ts * ns, d)
    zero_state = jnp.zeros((nb, nstate // LANES, LANES), F32)

    p_re, p_im, p_cv, p_ff = [], [], [], []
    s_re, s_im, s_cv, s_ff = [], [], [], []
    for l in range(depth):
        w_in_bf = w_in[l].astype(BF16)
        wglu_bf = s5_w_glu[l].astype(BF16)
        wout_bf = w_out[l].astype(BF16)
        wup_bf = ffn_w_up[l].astype(BF16)
        wdown_bf = ffn_w_down[l].astype(BF16)
        wpe_bf = pe_w[l].astype(BF16)
        wgate_bf = pe_w_gate[l].astype(BF16)
        ab_re, ab_im, bb_re, bb_im = _s5_discretise(s5_lam_re[l], s5_lam_im[l], s5_log_dt[l], s5_b_re[l], s5_b_im[l])
        wb, wc = _s5_block_weights(bb_re, bb_im, s5_c_re[l], s5_c_im[l])
        d_row = _row(s5_d[l])
        conv_w8 = jnp.repeat(conv_w[l].astype(F32), SUBLANES, axis=0)
        conv_b_row, cg_row, cb_row = _row(conv_b[l]), _row(conv_ln_g[l]), _row(conv_ln_b[l])
        fcw, fcb = ffn_conv_w[l].astype(F32), _row(ffn_conv_b[l])
        l1g, l1b, l2g, l2b, l3g, l3b = (_row(v[l]) for v in (ln1_g, ln1_b, ln2_g, ln2_b, ln3_g, ln3_b))

        u, c = _in_proj(xp, w_in_bf, s5w, cw, tm)
        y5, hre, him = _s5_seq(u, wb, wc, ab_re.reshape(nstate // LANES, LANES), ab_im.reshape(nstate // LANES, LANES),
                               d_row, zero_state, zero_state, nb, t, tc)
        ca = _conv_seq(c, conv_w8, conv_b_row, cg_row, cb_row, t, tconv, kk)
        x1 = _mix_out(y5, ca, xp, wglu_bf, wout_bf, l1g, l1b, alpha, tm)
        x2, tg, tv = _ffn(x1, wup_bf, wdown_bf, fcw, fcb, l2g, l2b, alpha, tm, tf, t=t)
        xp = _pe(x2, p_prompt[l].reshape(nb * t, -1), wgate_bf, wpe_bf, l3g, l3b, alpha, tm)
        p_re.append(hre.reshape(nb, groups, nstate_g))
        p_im.append(him.reshape(nb, groups, nstate_g))
        p_cv.append(c.reshape(nb, t, cw)[:, t - (kk - 1):])
        bps = t // tm
        tails = jnp.concatenate([tg, tv], axis=1).reshape(nb, bps, SUBLANES, 2 * dff)
        p_ff.append(tails[:, bps - 1, SUBLANES - (fk - 1):])

        u, c = _in_proj(xs, w_in_bf, s5w, cw, ts * ns)
        y5, hre, him = _s5_planes(u, wb, wc, ab_re.reshape(1, nstate), ab_im.reshape(1, nstate), d_row,
                                  state_s5_re[l].reshape(ns, nstate).astype(F32),
                                  state_s5_im[l].reshape(ns, nstate).astype(F32), ts, ns)
        cache_t = cache_conv[l].transpose(1, 0, 2)
        c_t = c.reshape(ts, ns, cw)
        ca = _conv_planes(cache_t, c_t, conv_w8, conv_b_row, cg_row, cb_row, kk, 32).reshape(ts * ns, cw)
        x1 = _mix_out(y5, ca, xs, wglu_bf, wout_bf, l1g, l1b, alpha, ts * ns)
        fcache = cache_ffn_conv[l].transpose(1, 0, 2).reshape((fk - 1) * ns, 2 * dff)
        x2, tg, tv = _ffn(x1, wup_bf, wdown_bf, fcw, fcb, l2g, l2b, alpha, ts * ns, tf, cache=fcache, nseq=ns)
        xs = _pe(x2, p_sample[l].transpose(1, 0, 2).reshape(ts * ns, -1), wgate_bf, wpe_bf, l3g, l3b, alpha, ts * ns)
        s_re.append(hre.reshape(ns, groups, nstate_g))
        s_im.append(him.reshape(ns, groups, nstate_g))
        s_cv.append(jnp.concatenate([cache_conv[l][:, ts:], c_t.transpose(1, 0, 2)], axis=1))
        s_ff.append(jnp.concatenate([tg, tv], axis=1).reshape(fk - 1, ns, 2 * dff).transpose(1, 0, 2))

    y_prompt = xp.reshape(nb, t, d)
    y_sample = xs.reshape(ts, ns, d).transpose(1, 0, 2)
    return (y_prompt, y_sample,
            jnp.stack(p_re), jnp.stack(p_im), jnp.stack(p_cv), jnp.stack(p_ff),
            jnp.stack(s_re), jnp.stack(s_im), jnp.stack(s_cv), jnp.stack(s_ff))
```

```python
import functools
import math

import jax
import jax.numpy as jnp
from jax import lax
from jax.experimental import pallas as pl
from jax.experimental.pallas import tpu as pltpu

F32 = jnp.float32
BF16 = jnp.bfloat16

LN_EPS = 1e-5
V7X_VMEM_LIMIT_BYTES = 56 * 1024 * 1024
SUBLANES = 8
LANES = 128
S5_GROUP_CH = 16
S5_STATE = 64
SUPER = 4


def _params(sem):
    return pltpu.CompilerParams(dimension_semantics=sem, vmem_limit_bytes=V7X_VMEM_LIMIT_BYTES)


def _resident(stacked, l):
    shape = stacked.shape[1:]
    nd = len(shape)
    return pl.BlockSpec((None,) + shape, lambda *_: (l,) + (0,) * nd, pipeline_mode=pl.Buffered(1))


def _ln(r, g, b):
    mu = jnp.mean(r, axis=-1, keepdims=True)
    d = r - mu
    var = jnp.mean(d * d, axis=-1, keepdims=True)
    return d * lax.rsqrt(var + LN_EPS) * g + b


def _in_proj_kernel(x_ref, w_ref, u_ref, c_ref, *, s5w, cw):
    xb = x_ref[...].astype(BF16)
    u_ref[...] = jnp.dot(xb, w_ref[:, :s5w], preferred_element_type=F32)
    cv = jnp.dot(xb, w_ref[:, s5w:s5w + cw], preferred_element_type=F32)
    cg = jnp.dot(xb, w_ref[:, s5w + cw:], preferred_element_type=F32)
    c_ref[...] = cv * jax.nn.sigmoid(cg)


def _in_proj(x, w_in_bf, l, s5w, cw, tm):
    n, d = x.shape
    return pl.pallas_call(
        functools.partial(_in_proj_kernel, s5w=s5w, cw=cw),
        grid=(n // tm,),
        in_specs=[pl.BlockSpec((tm, d), lambda i: (i, 0)), _resident(w_in_bf, l)],
        out_specs=[pl.BlockSpec((tm, s5w), lambda i: (i, 0)), pl.BlockSpec((tm, cw), lambda i: (i, 0))],
        out_shape=[jax.ShapeDtypeStruct((n, s5w), F32), jax.ShapeDtypeStruct((n, cw), F32)],
        compiler_params=_params(("arbitrary",)),
        name="in_proj",
    )(x, w_in_bf)


def _s5_seq_kernel(u_ref, wb_ref, wc_ref, are_ref, aim_ref, d_ref, h0re_ref, h0im_ref,
                   y_ref, hre_ref, him_ref, sre, sim, *, tc, sgw):
    step_rows = SUBLANES
    k = pl.program_id(1)

    @pl.when(k == 0)
    def _():
        hre_ref[...] = h0re_ref[...]
        him_ref[...] = h0im_ref[...]

    u = u_ref[...]
    ub = u.astype(BF16)
    for v in range(SUPER):
        r = jnp.dot(ub[:, v * sgw:(v + 1) * sgw], wb_ref[v], preferred_element_type=F32)
        for i in range(SUBLANES):
            sre[v, pl.ds(i, tc, stride=step_rows), :] = r[:, i * LANES:(i + 1) * LANES]
            sim[v, pl.ds(i, tc, stride=step_rows), :] = r[:, (SUBLANES + i) * LANES:(SUBLANES + i + 1) * LANES]

    a_re = [are_ref[v * SUBLANES:(v + 1) * SUBLANES, :] for v in range(SUPER)]
    a_im = [aim_ref[v * SUBLANES:(v + 1) * SUBLANES, :] for v in range(SUPER)]
    init = tuple(hre_ref[0, v * SUBLANES:(v + 1) * SUBLANES, :] for v in range(SUPER)) + \
        tuple(him_ref[0, v * SUBLANES:(v + 1) * SUBLANES, :] for v in range(SUPER))

    def step(t, carry):
        row = pl.multiple_of(t * step_rows, step_rows)
        new_re, new_im = [], []
        for v in range(SUPER):
            hr, hi = carry[v], carry[SUPER + v]
            nr = a_re[v] * hr - a_im[v] * hi + sre[v, pl.ds(row, step_rows), :]
            ni = a_re[v] * hi + a_im[v] * hr + sim[v, pl.ds(row, step_rows), :]
            sre[v, pl.ds(row, step_rows), :] = nr
            sim[v, pl.ds(row, step_rows), :] = ni
            new_re.append(nr)
            new_im.append(ni)
        return tuple(new_re) + tuple(new_im)

    fin = lax.fori_loop(0, tc, step, init, unroll=8)
    for v in range(SUPER):
        hre_ref[0, v * SUBLANES:(v + 1) * SUBLANES, :] = fin[v]
        him_ref[0, v * SUBLANES:(v + 1) * SUBLANES, :] = fin[SUPER + v]

    for v in range(SUPER):
        cols = [sre[v, pl.ds(i, tc, stride=step_rows), :] for i in range(SUBLANES)]
        cols += [sim[v, pl.ds(i, tc, stride=step_rows), :] for i in range(SUBLANES)]
        hcat = jnp.concatenate(cols, axis=1).astype(BF16)
        y = jnp.dot(hcat, wc_ref[v], preferred_element_type=F32)
        y_ref[:, v * sgw:(v + 1) * sgw] = y + d_ref[:, v * sgw:(v + 1) * sgw] * u[:, v * sgw:(v + 1) * sgw]


def _s5_seq(u, wb, wc, a_re, a_im, d, h0re, h0im, l, nb, t, tc):
    s5w = u.shape[1]
    sgw = s5w // SUPER
    nk = t // tc
    rows = a_re.shape[1]
    st_spec = pl.BlockSpec((1, rows, LANES), lambda b, k: (b, 0, 0))
    return pl.pallas_call(
        functools.partial(_s5_seq_kernel, tc=tc, sgw=sgw),
        grid=(nb, nk),
        in_specs=[pl.BlockSpec((tc, s5w), lambda b, k: (b * nk + k, 0)),
                  _resident(wb, l), _resident(wc, l), _resident(a_re, l), _resident(a_im, l),
                  _resident(d, l), st_spec, st_spec],
        out_specs=[pl.BlockSpec((tc, s5w), lambda b, k: (b * nk + k, 0)), st_spec, st_spec],
        out_shape=[jax.ShapeDtypeStruct((nb * t, s5w), F32),
                   jax.ShapeDtypeStruct((nb, rows, LANES), F32), jax.ShapeDtypeStruct((nb, rows, LANES), F32)],
        scratch_shapes=[pltpu.VMEM((SUPER, tc * SUBLANES, LANES), F32),
                        pltpu.VMEM((SUPER, tc * SUBLANES, LANES), F32)],
        compiler_params=_params(("arbitrary", "arbitrary")),
        name="s5_seq",
    )(u, wb, wc, a_re, a_im, d, h0re, h0im)


def _s5_planes_kernel(u_ref, wb_ref, wc_ref, are_ref, aim_ref, d_ref, h0re_ref, h0im_ref,
                      y_ref, hre_ref, him_ref, *, nt, nseq, half):
    u = u_ref[...]
    r = jnp.dot(u.astype(BF16), wb_ref[0], preferred_element_type=F32)
    a_re, a_im = are_ref[...], aim_ref[...]
    hr, hi = h0re_ref[...], h0im_ref[...]
    hs_re, hs_im = [], []
    for t in range(nt):
        br = r[t * nseq:(t + 1) * nseq, :half]
        bi = r[t * nseq:(t + 1) * nseq, half:]
        hr, hi = a_re * hr - a_im * hi + br, a_re * hi + a_im * hr + bi
        hs_re.append(hr)
        hs_im.append(hi)
    hre_ref[...] = hr
    him_ref[...] = hi
    hcat = jnp.concatenate([jnp.concatenate(hs_re, axis=0), jnp.concatenate(hs_im, axis=0)], axis=1)
    y = jnp.dot(hcat.astype(BF16), wc_ref[0], preferred_element_type=F32)
    y_ref[...] = y + d_ref[...] * u


def _s5_planes(u, wb, wc, a_re, a_im, d, h0re, h0im, l, nt, nseq):
    n, s5w = u.shape
    sgw = s5w // SUPER
    nstate = h0re.shape[2]
    half = nstate // SUPER
    return pl.pallas_call(
        functools.partial(_s5_planes_kernel, nt=nt, nseq=nseq, half=half),
        grid=(SUPER,),
        in_specs=[pl.BlockSpec((n, sgw), lambda v: (0, v)),
                  pl.BlockSpec((None, 1) + wb.shape[2:], lambda v: (l, v, 0, 0)),
                  pl.BlockSpec((None, 1) + wc.shape[2:], lambda v: (l, v, 0, 0)),
                  pl.BlockSpec((None, 1, half), lambda v: (l, 0, v)),
                  pl.BlockSpec((None, 1, half), lambda v: (l, 0, v)),
                  pl.BlockSpec((None, 1, sgw), lambda v: (l, 0, v)),
                  pl.BlockSpec((None, nseq, half), lambda v: (l, 0, v)),
                  pl.BlockSpec((None, nseq, half), lambda v: (l, 0, v))],
        out_specs=[pl.BlockSpec((n, sgw), lambda v: (0, v)),
                   pl.BlockSpec((nseq, half), lambda v: (0, v)), pl.BlockSpec((nseq, half), lambda v: (0, v))],
        out_shape=[jax.ShapeDtypeStruct((n, s5w), F32),
                   jax.ShapeDtypeStruct((nseq, nstate), F32), jax.ShapeDtypeStruct((nseq, nstate), F32)],
        compiler_params=_params(("arbitrary",)),
        name="s5_planes",
    )(u, wb, wc, a_re, a_im, d, h0re, h0im)


CONV_ROW_TILE = 32


def _conv_seq_kernel(c_ref, halo_ref, w_ref, b_ref, g_ref, bt_ref, out_ref, xp_ref, sh_ref,
                     *, tm, kk, halo, blocks_per_seq):
    i = pl.program_id(0)
    start = (i % blocks_per_seq) == 0
    xp_ref[0:halo, :] = jnp.where(start, 0.0, halo_ref[...])
    xp_ref[halo:halo + tm, :] = c_ref[...]
    ncopy = tm + halo - SUBLANES
    for b in range(1, SUBLANES):
        sh_ref[b - 1, :, :] = xp_ref[pl.ds(b, ncopy), :]
    rt = CONV_ROW_TILE
    bias = b_ref[...]
    gam, bet = g_ref[...], bt_ref[...]

    def body(r, carry):
        r0 = pl.multiple_of(r * rt, rt)
        acc = jnp.broadcast_to(bias, (rt, bias.shape[1]))
        for j in range(kk):
            a, b = divmod(halo - j, SUBLANES)
            k = kk - 1 - j
            if b == 0:
                xs = xp_ref[pl.ds(r0 + SUBLANES * a, rt), :]
            else:
                xs = sh_ref[b - 1, pl.ds(r0 + SUBLANES * a, rt), :]
            w8 = w_ref[SUBLANES * k:SUBLANES * (k + 1), :]
            acc = acc + jnp.concatenate([w8] * (rt // SUBLANES), axis=0) * xs
        out_ref[pl.ds(r0, rt), :] = jax.nn.silu(_ln(acc, gam, bet))
        return carry

    lax.fori_loop(0, tm // rt, body, 0)


def _conv_seq(c, w8, b, g, bt, l, t, tm, kk):
    n, cw = c.shape
    halo = 32
    assert kk - 1 <= halo and tm % halo == 0 and t % tm == 0
    hb = tm // halo
    return pl.pallas_call(
        functools.partial(_conv_seq_kernel, tm=tm, kk=kk, halo=halo, blocks_per_seq=t // tm),
        grid=(n // tm,),
        in_specs=[pl.BlockSpec((tm, cw), lambda i: (i, 0)),
                  pl.BlockSpec((halo, cw), lambda i: (jnp.maximum(i * hb - 1, 0), 0)),
                  _resident(w8, l), _resident(b, l), _resident(g, l), _resident(bt, l)],
        out_specs=pl.BlockSpec((tm, cw), lambda i: (i, 0)),
        out_shape=jax.ShapeDtypeStruct((n, cw), F32),
        scratch_shapes=[pltpu.VMEM((tm + halo, cw), F32),
                        pltpu.VMEM((SUBLANES - 1, tm + halo - SUBLANES, cw), F32)],
        compiler_params=_params(("arbitrary",)),
        name="conv_seq",
    )(c, c, w8, b, g, bt)


def _conv_planes_kernel(cache_ref, c_ref, w_ref, b_ref, g_ref, bt_ref, out_ref, *, nt, kk):
    rows, cw = c_ref.shape[1], c_ref.shape[2]
    gam, bet = g_ref[...], bt_ref[...]
    for t in range(nt):
        acc = jnp.broadcast_to(b_ref[...], (rows, cw))
        for k in range(kk):
            p = t + k
            xs = cache_ref[p] if p < kk - 1 else c_ref[p - (kk - 1)]
            w8 = w_ref[SUBLANES * k:SUBLANES * (k + 1), :]
            acc = acc + jnp.concatenate([w8] * (rows // SUBLANES), axis=0) * xs
        out_ref[t] = jax.nn.silu(_ln(acc, gam, bet))


def _conv_planes(cache_t, c_t, w8, b, g, bt, l, kk, rows):
    nt, nseq, cw = c_t.shape
    return pl.pallas_call(
        functools.partial(_conv_planes_kernel, nt=nt, kk=kk),
        grid=(nseq // rows,),
        in_specs=[pl.BlockSpec((None, kk - 1, rows, cw), lambda i: (l, 0, i, 0)),
                  pl.BlockSpec((nt, rows, cw), lambda i: (0, i, 0)),
                  _resident(w8, l), _resident(b, l), _resident(g, l), _resident(bt, l)],
        out_specs=pl.BlockSpec((nt, rows, cw), lambda i: (0, i, 0)),
        out_shape=jax.ShapeDtypeStruct((nt, nseq, cw), F32),
        compiler_params=_params(("arbitrary",)),
        name="conv_planes",
    )(cache_t, c_t, w8, b, g, bt)


def _mix_out_kernel(y5_ref, ca_ref, x_ref, wglu_ref, wout_ref, g_ref, b_ref, out_ref, *, alpha, s5w):
    g5 = jax.nn.gelu(y5_ref[...])
    gate = jnp.dot(g5.astype(BF16), wglu_ref[...], preferred_element_type=F32)
    s5o = g5 * jax.nn.sigmoid(gate)
    mix = jnp.dot(s5o.astype(BF16), wout_ref[:s5w, :], preferred_element_type=F32)
    mix = mix + jnp.dot(ca_ref[...].astype(BF16), wout_ref[s5w:, :], preferred_element_type=F32)
    out_ref[...] = _ln(alpha * x_ref[...] + mix, g_ref[...], b_ref[...])


def _mix_out(y5, ca, x, wglu, wout, g, b, l, alpha, tm):
    n, d = x.shape
    s5w, cw = y5.shape[1], ca.shape[1]
    return pl.pallas_call(
        functools.partial(_mix_out_kernel, alpha=alpha, s5w=s5w),
        grid=(n // tm,),
        in_specs=[pl.BlockSpec((tm, s5w), lambda i: (i, 0)), pl.BlockSpec((tm, cw), lambda i: (i, 0)),
                  pl.BlockSpec((tm, d), lambda i: (i, 0)),
                  _resident(wglu, l), _resident(wout, l), _resident(g, l), _resident(b, l)],
        out_specs=pl.BlockSpec((tm, d), lambda i: (i, 0)),
        out_shape=jax.ShapeDtypeStruct((n, d), F32),
        compiler_params=_params(("arbitrary",)),
        name="mix_out",
    )(y5, ca, x, wglu, wout, g, b)


FFN_HALO = 16


def _ffn_kernel(*refs, alpha, tm, planes, nseq, blocks_per_seq):
    if planes:
        (x_ref, hg_c_ref, hv_c_ref, wg_ref, wv_ref, wd_ref, cwg_ref, cwv_ref, cbg_ref, cbv_ref, g_ref, b_ref,
         out_ref, tg_ref, tv_ref, xb_ref, acc_ref) = refs
    else:
        (x_ref, halo_ref, wg_ref, wv_ref, wd_ref, cwg_ref, cwv_ref, cbg_ref, cbv_ref, g_ref, b_ref,
         out_ref, tg_ref, tv_ref, xb_ref, acc_ref) = refs
    i = pl.program_id(0)
    j = pl.program_id(1)

    @pl.when(j == 0)
    def _():
        if planes:
            xb_ref[...] = x_ref[...].astype(BF16)
        else:
            start = (i % blocks_per_seq) == 0
            xb_ref[0:FFN_HALO, :] = jnp.where(start, 0.0, halo_ref[...]).astype(BF16)
            xb_ref[FFN_HALO:, :] = x_ref[...].astype(BF16)
        acc_ref[...] = jnp.zeros_like(acc_ref)

    xb = xb_ref[...]
    hg = jnp.dot(xb, wg_ref[...], preferred_element_type=F32)
    hv = jnp.dot(xb, wv_ref[...], preferred_element_type=F32)

    def conv3(h, hist_ref, cw_ref, cb_ref, tail_ref):
        w0, w1, w2 = cw_ref[0:1, :], cw_ref[1:2, :], cw_ref[2:3, :]
        if planes:
            hp = jnp.concatenate([hist_ref[...], h], axis=0)
            cur, prev1, prev2 = hp[2 * nseq:], hp[nseq:nseq + tm], hp[:tm]
            tail_ref[...] = h[tm - 2 * nseq:, :]
        else:
            cur = h[FFN_HALO:, :]
            prev1 = h[FFN_HALO - 1:FFN_HALO - 1 + tm, :]
            prev2 = h[FFN_HALO - 2:FFN_HALO - 2 + tm, :]
            tail_ref[...] = h[FFN_HALO + tm - SUBLANES:, :]
        return w2 * cur + w1 * prev1 + w0 * prev2 + cb_ref[...]

    cg = conv3(hg, hg_c_ref if planes else None, cwg_ref, cbg_ref, tg_ref)
    cv = conv3(hv, hv_c_ref if planes else None, cwv_ref, cbv_ref, tv_ref)
    act = (jax.nn.silu(cg) * cv).astype(BF16)
    acc_ref[...] += jnp.dot(act, wd_ref[...], preferred_element_type=F32)

    @pl.when(j == pl.num_programs(1) - 1)
    def _():
        out_ref[...] = _ln(alpha * x_ref[...] + acc_ref[...], g_ref[...], b_ref[...])


def _ffn(x, w_up, w_down, conv_w, conv_b, g, b, l, alpha, tm, tf, *, cache=None, t=None, nseq=None):
    n, d = x.shape
    dff = w_down.shape[1]
    nj = dff // tf
    fk = conv_w.shape[1]
    planes = cache is not None
    wspecs = [pl.BlockSpec((None, d, tf), lambda i, j: (l, 0, j)),
              pl.BlockSpec((None, d, tf), lambda i, j: (l, 0, nj + j)),
              pl.BlockSpec((None, tf, d), lambda i, j: (l, j, 0)),
              pl.BlockSpec((None, fk, tf), lambda i, j: (l, 0, j)),
              pl.BlockSpec((None, fk, tf), lambda i, j: (l, 0, nj + j)),
              pl.BlockSpec((None, 1, tf), lambda i, j: (l, 0, j)),
              pl.BlockSpec((None, 1, tf), lambda i, j: (l, 0, nj + j)),
              pl.BlockSpec((None, 1, d), lambda i, j: (l, 0, 0)), pl.BlockSpec((None, 1, d), lambda i, j: (l, 0, 0))]
    wargs = [w_up, w_up, w_down, conv_w, conv_w, conv_b, conv_b, g, b]
    if planes:
        assert n == tm
        hist = cache.shape[1]
        in_specs = [pl.BlockSpec((tm, d), lambda i, j: (i, 0)),
                    pl.BlockSpec((None, hist, tf), lambda i, j: (l, 0, j)),
                    pl.BlockSpec((None, hist, tf), lambda i, j: (l, 0, nj + j))] + wspecs
        args = [x, cache, cache] + wargs
        tail_rows, xb_rows, bps = hist, tm, 1
    else:
        hb = tm // FFN_HALO
        in_specs = [pl.BlockSpec((tm, d), lambda i, j: (i, 0)),
                    pl.BlockSpec((FFN_HALO, d), lambda i, j: (jnp.maximum(i * hb - 1, 0), 0))] + wspecs
        args = [x, x] + wargs
        tail_rows, xb_rows, bps = SUBLANES, tm + FFN_HALO, t // tm
    nblk = n // tm
    tail_spec = pl.BlockSpec((tail_rows, tf), lambda i, j: (i, j))
    return pl.pallas_call(
        functools.partial(_ffn_kernel, alpha=alpha, tm=tm, planes=planes, nseq=nseq, blocks_per_seq=bps),
        grid=(nblk, nj),
        in_specs=in_specs,
        out_specs=[pl.BlockSpec((tm, d), lambda i, j: (i, 0)), tail_spec, tail_spec],
        out_shape=[jax.ShapeDtypeStruct((n, d), F32),
                   jax.ShapeDtypeStruct((nblk * tail_rows, dff), F32),
                   jax.ShapeDtypeStruct((nblk * tail_rows, dff), F32)],
        scratch_shapes=[pltpu.VMEM((xb_rows, d), BF16), pltpu.VMEM((tm, d), F32)],
        compiler_params=_params(("arbitrary", "arbitrary")),
        name="ffn_planes" if planes else "ffn_seq",
    )(*args)


def _pe_kernel(x_ref, p_ref, wg_ref, wp_ref, g_ref, b_ref, out_ref, *, alpha):
    x = x_ref[...]
    gate = jax.nn.sigmoid(jnp.dot(x.astype(BF16), wg_ref[...], preferred_element_type=F32))
    e = jnp.dot(p_ref[...].astype(BF16), wp_ref[...], preferred_element_type=F32)
    out_ref[...] = _ln(alpha * x + gate * e, g_ref[...], b_ref[...])


def _pe(x, p, wg, wp, g, b, l, alpha, tm):
    n, d = x.shape
    pd = p.shape[2]
    return pl.pallas_call(
        functools.partial(_pe_kernel, alpha=alpha),
        grid=(n // tm,),
        in_specs=[pl.BlockSpec((tm, d), lambda i: (i, 0)), pl.BlockSpec((None, tm, pd), lambda i: (l, i, 0)),
                  _resident(wg, l), _resident(wp, l), _resident(g, l), _resident(b, l)],
        out_specs=pl.BlockSpec((tm, d), lambda i: (i, 0)),
        out_shape=jax.ShapeDtypeStruct((n, d), F32),
        compiler_params=_params(("arbitrary",)),
        name="pe_embed",
    )(x, p, wg, wp, g, b)


def _s5_discretise(lam_re, lam_im, log_dt, b_re, b_im):
    lr, li = lam_re.astype(F32), lam_im.astype(F32)
    dt = jnp.exp(log_dt.astype(F32))[..., None]
    mag = jnp.exp(lr * dt)
    ab_re = mag * jnp.cos(li * dt)
    ab_im = mag * jnp.sin(li * dt)
    num_re, num_im = ab_re - 1.0, ab_im
    den = lr * lr + li * li
    q_re = (num_re * lr + num_im * li) / den
    q_im = (num_im * lr - num_re * li) / den
    br, bi = b_re.astype(F32), b_im.astype(F32)
    bb_re = q_re[..., None] * br - q_im[..., None] * bi
    bb_im = q_re[..., None] * bi + q_im[..., None] * br
    return ab_re, ab_im, bb_re, bb_im


def _s5_block_weights(bb_re, bb_im, c_re, c_im):
    nl, g, p, h = bb_re.shape
    gs = g // SUPER
    eye = jnp.eye(gs, dtype=F32)

    def b_blocks(bb):
        t = bb.reshape(nl, SUPER, gs, p, h)
        return jnp.einsum('lsgph,gk->lsghkp', t, eye).reshape(nl, SUPER, gs * h, gs * p)

    def c_blocks(c):
        t = c.astype(F32).reshape(nl, SUPER, gs, h, p)
        return jnp.einsum('lsghp,gk->lsgpkh', t, eye).reshape(nl, SUPER, gs * p, gs * h)

    wb = jnp.concatenate([b_blocks(bb_re), b_blocks(bb_im)], axis=3).astype(BF16)
    wc = jnp.concatenate([c_blocks(c_re), -c_blocks(c_im)], axis=2).astype(BF16)
    return wb, wc


def _rows(v):
    return v.astype(F32).reshape(v.shape[0], 1, -1)


def kernel(x_prompt, x_sample, state_s5_re, state_s5_im, cache_conv, cache_ffn_conv, p_prompt, p_sample,
           w_in, s5_lam_re, s5_lam_im, s5_log_dt, s5_b_re, s5_b_im, s5_c_re, s5_c_im, s5_d, s5_w_glu,
           conv_w, conv_b, conv_ln_g, conv_ln_b, w_out, ln1_g, ln1_b,
           ffn_w_up, ffn_conv_w, ffn_conv_b, ffn_w_down, ln2_g, ln2_b,
           pe_w, pe_w_gate, ln3_g, ln3_b):
    nb, t, d = x_prompt.shape
    ns, ts, _ = x_sample.shape
    depth = w_in.shape[0]
    groups, nstate_g = s5_lam_re.shape[1], s5_lam_re.shape[2]
    s5w = groups * s5_d.shape[2]
    cw = d - s5w
    nstate = groups * nstate_g
    kk = conv_w.shape[1]
    fk = ffn_conv_w.shape[1]
    dff = ffn_w_down.shape[1]
    assert fk == 3 and ts >= fk - 1 and ts <= kk - 1 and t >= kk - 1
    alpha = (2.0 * depth) ** 0.25
    tm, tf, tc, tconv = 512, 512, 256, 256

    xp = x_prompt.reshape(nb * t, d)
    xs = x_sample.transpose(1, 0, 2).reshape(ts * ns, d)
    zero_state = jnp.zeros((nb, nstate // LANES, LANES), F32)

    w_in_bf, wglu_bf, wout_bf = w_in.astype(BF16), s5_w_glu.astype(BF16), w_out.astype(BF16)
    wup_bf, wdown_bf = ffn_w_up.astype(BF16), ffn_w_down.astype(BF16)
    wpe_bf, wgate_bf = pe_w.astype(BF16), pe_w_gate.astype(BF16)
    ab_re, ab_im, bb_re, bb_im = _s5_discretise(s5_lam_re, s5_lam_im, s5_log_dt, s5_b_re, s5_b_im)
    wb, wc = _s5_block_weights(bb_re, bb_im, s5_c_re, s5_c_im)
    a_tile_re, a_tile_im = (v.reshape(depth, nstate // LANES, LANES) for v in (ab_re, ab_im))
    a_row_re, a_row_im = (v.reshape(depth, 1, nstate) for v in (ab_re, ab_im))
    d_rows = _rows(s5_d.reshape(depth, s5w))
    conv_w8 = jnp.repeat(conv_w.astype(F32), SUBLANES, axis=1)
    conv_b_rows, cg_rows, cb_rows = _rows(conv_b), _rows(conv_ln_g), _rows(conv_ln_b)
    fcw, fcb = ffn_conv_w.astype(F32), _rows(ffn_conv_b)
    l1g, l1b, l2g, l2b, l3g, l3b = (_rows(v) for v in (ln1_g, ln1_b, ln2_g, ln2_b, ln3_g, ln3_b))
    pp = p_prompt.reshape(depth, nb * t, -1)
    ps = p_sample.transpose(0, 2, 1, 3).reshape(depth, ts * ns, -1)
    h0_re = state_s5_re.reshape(depth, ns, nstate).astype(F32)
    h0_im = state_s5_im.reshape(depth, ns, nstate).astype(F32)
    cache_t = cache_conv.transpose(0, 2, 1, 3)
    fcache = cache_ffn_conv.transpose(0, 2, 1, 3).reshape(depth, (fk - 1) * ns, 2 * dff)

    p_re, p_im, p_cv, p_ff = [], [], [], []
    s_re, s_im, s_cv, s_ff = [], [], [], []
    for l in range(depth):
        u, c = _in_proj(xp, w_in_bf, l, s5w, cw, tm)
        y5, hre, him = _s5_seq(u, wb, wc, a_tile_re, a_tile_im, d_rows, zero_state, zero_state, l, nb, t, tc)
        ca = _conv_seq(c, conv_w8, conv_b_rows, cg_rows, cb_rows, l, t, tconv, kk)
        x1 = _mix_out(y5, ca, xp, wglu_bf, wout_bf, l1g, l1b, l, alpha, tm)
        x2, tg, tv = _ffn(x1, wup_bf, wdown_bf, fcw, fcb, l2g, l2b, l, alpha, tm, tf, t=t)
        xp = _pe(x2, pp, wgate_bf, wpe_bf, l3g, l3b, l, alpha, tm)
        p_re.append(hre.reshape(nb, groups, nstate_g))
        p_im.append(him.reshape(nb, groups, nstate_g))
        p_cv.append(c.reshape(nb, t, cw)[:, t - (kk - 1):])
        bps = t // tm
        tails = jnp.concatenate([tg, tv], axis=1).reshape(nb, bps, SUBLANES, 2 * dff)
        p_ff.append(tails[:, bps - 1, SUBLANES - (fk - 1):])

        u, c = _in_proj(xs, w_in_bf, l, s5w, cw, ts * ns)
        y5, hre, him = _s5_planes(u, wb, wc, a_row_re, a_row_im, d_rows, h0_re, h0_im, l, ts, ns)
        c_t = c.reshape(ts, ns, cw)
        ca = _conv_planes(cache_t, c_t, conv_w8, conv_b_rows, cg_rows, cb_rows, l, kk, 32).reshape(ts * ns, cw)
        x1 = _mix_out(y5, ca, xs, wglu_bf, wout_bf, l1g, l1b, l, alpha, ts * ns)
        x2, tg, tv = _ffn(x1, wup_bf, wdown_bf, fcw, fcb, l2g, l2b, l, alpha, ts * ns, tf, cache=fcache, nseq=ns)
        xs = _pe(x2, ps, wgate_bf, wpe_bf, l3g, l3b, l, alpha, ts * ns)
        s_re.append(hre.reshape(ns, groups, nstate_g))
        s_im.append(him.reshape(ns, groups, nstate_g))
        s_cv.append(jnp.concatenate([cache_conv[l][:, ts:], c_t.transpose(1, 0, 2)], axis=1))
        s_ff.append(jnp.concatenate([tg, tv], axis=1).reshape(fk - 1, ns, 2 * dff).transpose(1, 0, 2))

    y_prompt = xp.reshape(nb, t, d)
    y_sample = xs.reshape(ts, ns, d).transpose(1, 0, 2)
    return (y_prompt, y_sample,
            jnp.stack(p_re), jnp.stack(p_im), jnp.stack(p_cv), jnp.stack(p_ff),
            jnp.stack(s_re), jnp.stack(s_im), jnp.stack(s_cv), jnp.stack(s_ff))
```

```python
import functools
import math

import jax
import jax.numpy as jnp
from jax import lax
from jax.experimental import pallas as pl
from jax.experimental.pallas import tpu as pltpu

F32 = jnp.float32
BF16 = jnp.bfloat16

LN_EPS = 1e-5
V7X_VMEM_LIMIT_BYTES = 56 * 1024 * 1024
SUBLANES = 8
LANES = 128
S5_GROUP_CH = 16
S5_STATE = 64
SUPER = 4
ROW_SPLIT = 2


def _params(sem):
    return pltpu.CompilerParams(dimension_semantics=sem, vmem_limit_bytes=V7X_VMEM_LIMIT_BYTES)


def _resident(stacked, l):
    shape = stacked.shape[1:]
    nd = len(shape)
    return pl.BlockSpec((None,) + shape, lambda *_: (l,) + (0,) * nd, pipeline_mode=pl.Buffered(1))


def _ln(r, g, b):
    mu = jnp.mean(r, axis=-1, keepdims=True)
    d = r - mu
    var = jnp.mean(d * d, axis=-1, keepdims=True)
    return d * lax.rsqrt(var + LN_EPS) * g + b


def _in_proj_kernel(x_ref, w_ref, u_ref, c_ref, *, s5w, cw):
    rs = x_ref.shape[0] // ROW_SPLIT
    for s in range(ROW_SPLIT):
        rows = slice(s * rs, (s + 1) * rs)
        xb = x_ref[rows, :].astype(BF16)
        u_ref[rows, :] = jnp.dot(xb, w_ref[:, :s5w], preferred_element_type=F32)
        cv = jnp.dot(xb, w_ref[:, s5w:s5w + cw], preferred_element_type=F32)
        cg = jnp.dot(xb, w_ref[:, s5w + cw:], preferred_element_type=F32)
        c_ref[rows, :] = cv * jax.nn.sigmoid(cg)


def _in_proj(x, w_in_bf, l, s5w, cw, tm):
    n, d = x.shape
    return pl.pallas_call(
        functools.partial(_in_proj_kernel, s5w=s5w, cw=cw),
        grid=(n // tm,),
        in_specs=[pl.BlockSpec((tm, d), lambda i: (i, 0)), _resident(w_in_bf, l)],
        out_specs=[pl.BlockSpec((tm, s5w), lambda i: (i, 0)), pl.BlockSpec((tm, cw), lambda i: (i, 0))],
        out_shape=[jax.ShapeDtypeStruct((n, s5w), F32), jax.ShapeDtypeStruct((n, cw), F32)],
        compiler_params=_params(("arbitrary",)),
        name="in_proj",
    )(x, w_in_bf)


def _s5_seq_kernel(u_ref, wb_ref, wc_ref, are_ref, aim_ref, d_ref, h0re_ref, h0im_ref,
                   y_ref, hre_ref, him_ref, sre, sim, *, tc, sgw):
    step_rows = SUBLANES
    k = pl.program_id(1)

    @pl.when(k == 0)
    def _():
        hre_ref[...] = h0re_ref[...]
        him_ref[...] = h0im_ref[...]

    u = u_ref[...]
    ub = u.astype(BF16)
    for v in range(SUPER):
        r = jnp.dot(ub[:, v * sgw:(v + 1) * sgw], wb_ref[v], preferred_element_type=F32)
        for i in range(SUBLANES):
            sre[v, pl.ds(i, tc, stride=step_rows), :] = r[:, i * LANES:(i + 1) * LANES]
            sim[v, pl.ds(i, tc, stride=step_rows), :] = r[:, (SUBLANES + i) * LANES:(SUBLANES + i + 1) * LANES]

    a_re = [are_ref[v * SUBLANES:(v + 1) * SUBLANES, :] for v in range(SUPER)]
    a_im = [aim_ref[v * SUBLANES:(v + 1) * SUBLANES, :] for v in range(SUPER)]
    init = tuple(hre_ref[0, v * SUBLANES:(v + 1) * SUBLANES, :] for v in range(SUPER)) + \
        tuple(him_ref[0, v * SUBLANES:(v + 1) * SUBLANES, :] for v in range(SUPER))

    def step(t, carry):
        row = pl.multiple_of(t * step_rows, step_rows)
        new_re, new_im = [], []
        for v in range(SUPER):
            hr, hi = carry[v], carry[SUPER + v]
            nr = a_re[v] * hr - a_im[v] * hi + sre[v, pl.ds(row, step_rows), :]
            ni = a_re[v] * hi + a_im[v] * hr + sim[v, pl.ds(row, step_rows), :]
            sre[v, pl.ds(row, step_rows), :] = nr
            sim[v, pl.ds(row, step_rows), :] = ni
            new_re.append(nr)
            new_im.append(ni)
        return tuple(new_re) + tuple(new_im)

    fin = lax.fori_loop(0, tc, step, init, unroll=8)
    for v in range(SUPER):
        hre_ref[0, v * SUBLANES:(v + 1) * SUBLANES, :] = fin[v]
        him_ref[0, v * SUBLANES:(v + 1) * SUBLANES, :] = fin[SUPER + v]

    for v in range(SUPER):
        cols = [sre[v, pl.ds(i, tc, stride=step_rows), :] for i in range(SUBLANES)]
        cols += [sim[v, pl.ds(i, tc, stride=step_rows), :] for i in range(SUBLANES)]
        hcat = jnp.concatenate(cols, axis=1).astype(BF16)
        y = jnp.dot(hcat, wc_ref[v], preferred_element_type=F32)
        y_ref[:, v * sgw:(v + 1) * sgw] = y + d_ref[:, v * sgw:(v + 1) * sgw] * u[:, v * sgw:(v + 1) * sgw]


def _s5_seq(u, wb, wc, a_re, a_im, d, h0re, h0im, l, nb, t, tc):
    s5w = u.shape[1]
    sgw = s5w // SUPER
    nk = t // tc
    rows = a_re.shape[1]
    st_spec = pl.BlockSpec((1, rows, LANES), lambda b, k: (b, 0, 0))
    return pl.pallas_call(
        functools.partial(_s5_seq_kernel, tc=tc, sgw=sgw),
        grid=(nb, nk),
        in_specs=[pl.BlockSpec((tc, s5w), lambda b, k: (b * nk + k, 0)),
                  _resident(wb, l), _resident(wc, l), _resident(a_re, l), _resident(a_im, l),
                  _resident(d, l), st_spec, st_spec],
        out_specs=[pl.BlockSpec((tc, s5w), lambda b, k: (b * nk + k, 0)), st_spec, st_spec],
        out_shape=[jax.ShapeDtypeStruct((nb * t, s5w), F32),
                   jax.ShapeDtypeStruct((nb, rows, LANES), F32), jax.ShapeDtypeStruct((nb, rows, LANES), F32)],
        scratch_shapes=[pltpu.VMEM((SUPER, tc * SUBLANES, LANES), F32),
                        pltpu.VMEM((SUPER, tc * SUBLANES, LANES), F32)],
        compiler_params=_params(("arbitrary", "arbitrary")),
        name="s5_seq",
    )(u, wb, wc, a_re, a_im, d, h0re, h0im)


def _s5_planes_kernel(u_ref, wb_ref, wc_ref, are_ref, aim_ref, d_ref, h0re_ref, h0im_ref,
                      y_ref, hre_ref, him_ref, *, nt, nseq, half):
    u = u_ref[...]
    r = jnp.dot(u.astype(BF16), wb_ref[0], preferred_element_type=F32)
    a_re, a_im = are_ref[...], aim_ref[...]
    hr, hi = h0re_ref[...], h0im_ref[...]
    hs_re, hs_im = [], []
    for t in range(nt):
        br = r[t * nseq:(t + 1) * nseq, :half]
        bi = r[t * nseq:(t + 1) * nseq, half:]
        hr, hi = a_re * hr - a_im * hi + br, a_re * hi + a_im * hr + bi
        hs_re.append(hr)
        hs_im.append(hi)
    hre_ref[...] = hr
    him_ref[...] = hi
    hcat = jnp.concatenate([jnp.concatenate(hs_re, axis=0), jnp.concatenate(hs_im, axis=0)], axis=1)
    y = jnp.dot(hcat.astype(BF16), wc_ref[0], preferred_element_type=F32)
    y_ref[...] = y + d_ref[...] * u


def _s5_planes(u, wb, wc, a_re, a_im, d, h0re, h0im, l, nt, nseq):
    n, s5w = u.shape
    sgw = s5w // SUPER
    nstate = h0re.shape[2]
    half = nstate // SUPER
    return pl.pallas_call(
        functools.partial(_s5_planes_kernel, nt=nt, nseq=nseq, half=half),
        grid=(SUPER,),
        in_specs=[pl.BlockSpec((n, sgw), lambda v: (0, v)),
                  pl.BlockSpec((None, 1) + wb.shape[2:], lambda v: (l, v, 0, 0)),
                  pl.BlockSpec((None, 1) + wc.shape[2:], lambda v: (l, v, 0, 0)),
                  pl.BlockSpec((None, 1, half), lambda v: (l, 0, v)),
                  pl.BlockSpec((None, 1, half), lambda v: (l, 0, v)),
                  pl.BlockSpec((None, 1, sgw), lambda v: (l, 0, v)),
                  pl.BlockSpec((None, nseq, half), lambda v: (l, 0, v)),
                  pl.BlockSpec((None, nseq, half), lambda v: (l, 0, v))],
        out_specs=[pl.BlockSpec((n, sgw), lambda v: (0, v)),
                   pl.BlockSpec((nseq, half), lambda v: (0, v)), pl.BlockSpec((nseq, half), lambda v: (0, v))],
        out_shape=[jax.ShapeDtypeStruct((n, s5w), F32),
                   jax.ShapeDtypeStruct((nseq, nstate), F32), jax.ShapeDtypeStruct((nseq, nstate), F32)],
        compiler_params=_params(("arbitrary",)),
        name="s5_planes",
    )(u, wb, wc, a_re, a_im, d, h0re, h0im)


CONV_ROW_TILE = 32
CONV_COL_TILE = 512


def _conv_seq_kernel(c_ref, halo_ref, w_ref, b_ref, g_ref, bt_ref, out_ref, xp_ref, sh_ref, y_ref,
                     *, tm, kk, halo, blocks_per_seq):
    i = pl.program_id(0)
    start = (i % blocks_per_seq) == 0
    xp_ref[0:halo, :] = jnp.where(start, 0.0, halo_ref[...])
    xp_ref[halo:halo + tm, :] = c_ref[...]
    ncopy = tm + halo - SUBLANES
    for b in range(1, SUBLANES):
        sh_ref[b - 1, :, :] = xp_ref[pl.ds(b, ncopy), :]
    rt = CONV_ROW_TILE
    for lc in range(c_ref.shape[1] // CONV_COL_TILE):
        cols = slice(lc * CONV_COL_TILE, (lc + 1) * CONV_COL_TILE)

        def tile(r, carry, cols=cols):
            r0 = pl.multiple_of(r * rt, rt)
            acc = jnp.broadcast_to(b_ref[:, cols], (rt, CONV_COL_TILE))
            for j in range(kk):
                a, b = divmod(halo - j, SUBLANES)
                k = kk - 1 - j
                rows = pl.ds(r0 + SUBLANES * a, rt)
                xs = xp_ref[rows, cols] if b == 0 else sh_ref[b - 1, rows, cols]
                acc = acc + w_ref[k:k + 1, cols] * xs
            y_ref[pl.ds(r0, rt), cols] = acc
            return carry

        lax.fori_loop(0, tm // rt, tile, 0)
    out_ref[...] = jax.nn.silu(_ln(y_ref[...], g_ref[...], bt_ref[...]))


def _conv_seq(c, w, b, g, bt, l, t, tm, kk):
    n, cw = c.shape
    halo = 32
    assert kk - 1 <= halo and tm % halo == 0 and t % tm == 0
    hb = tm // halo
    return pl.pallas_call(
        functools.partial(_conv_seq_kernel, tm=tm, kk=kk, halo=halo, blocks_per_seq=t // tm),
        grid=(n // tm,),
        in_specs=[pl.BlockSpec((tm, cw), lambda i: (i, 0)),
                  pl.BlockSpec((halo, cw), lambda i: (jnp.maximum(i * hb - 1, 0), 0)),
                  _resident(w, l), _resident(b, l), _resident(g, l), _resident(bt, l)],
        out_specs=pl.BlockSpec((tm, cw), lambda i: (i, 0)),
        out_shape=jax.ShapeDtypeStruct((n, cw), F32),
        scratch_shapes=[pltpu.VMEM((tm + halo, cw), F32),
                        pltpu.VMEM((SUBLANES - 1, tm + halo - SUBLANES, cw), F32),
                        pltpu.VMEM((tm, cw), F32)],
        compiler_params=_params(("arbitrary",)),
        name="conv_seq",
    )(c, c, w, b, g, bt)


def _conv_planes_kernel(cache_ref, c_ref, w_ref, b_ref, g_ref, bt_ref, out_ref, *, nt, kk):
    rows, cw = c_ref.shape[1], c_ref.shape[2]
    gam, bet = g_ref[...], bt_ref[...]
    for t in range(nt):
        acc = jnp.broadcast_to(b_ref[...], (rows, cw))
        for k in range(kk):
            p = t + k
            xs = cache_ref[p] if p < kk - 1 else c_ref[p - (kk - 1)]
            w8 = w_ref[SUBLANES * k:SUBLANES * (k + 1), :]
            acc = acc + jnp.concatenate([w8] * (rows // SUBLANES), axis=0) * xs
        out_ref[t] = jax.nn.silu(_ln(acc, gam, bet))


def _conv_planes(cache_t, c_t, w8, b, g, bt, l, kk, rows):
    nt, nseq, cw = c_t.shape
    return pl.pallas_call(
        functools.partial(_conv_planes_kernel, nt=nt, kk=kk),
        grid=(nseq // rows,),
        in_specs=[pl.BlockSpec((None, kk - 1, rows, cw), lambda i: (l, 0, i, 0)),
                  pl.BlockSpec((nt, rows, cw), lambda i: (0, i, 0)),
                  _resident(w8, l), _resident(b, l), _resident(g, l), _resident(bt, l)],
        out_specs=pl.BlockSpec((nt, rows, cw), lambda i: (0, i, 0)),
        out_shape=jax.ShapeDtypeStruct((nt, nseq, cw), F32),
        compiler_params=_params(("arbitrary",)),
        name="conv_planes",
    )(cache_t, c_t, w8, b, g, bt)


def _mix_out_kernel(y5_ref, ca_ref, x_ref, wglu_ref, wout_ref, g_ref, b_ref, out_ref, *, alpha, s5w):
    rs = x_ref.shape[0] // ROW_SPLIT
    for s in range(ROW_SPLIT):
        rows = slice(s * rs, (s + 1) * rs)
        g5 = jax.nn.gelu(y5_ref[rows, :])
        gate = jnp.dot(g5.astype(BF16), wglu_ref[...], preferred_element_type=F32)
        s5o = g5 * jax.nn.sigmoid(gate)
        mix = jnp.dot(s5o.astype(BF16), wout_ref[:s5w, :], preferred_element_type=F32)
        mix = mix + jnp.dot(ca_ref[rows, :].astype(BF16), wout_ref[s5w:, :], preferred_element_type=F32)
        out_ref[rows, :] = _ln(alpha * x_ref[rows, :] + mix, g_ref[...], b_ref[...])


def _mix_out(y5, ca, x, wglu, wout, g, b, l, alpha, tm):
    n, d = x.shape
    s5w, cw = y5.shape[1], ca.shape[1]
    return pl.pallas_call(
        functools.partial(_mix_out_kernel, alpha=alpha, s5w=s5w),
        grid=(n // tm,),
        in_specs=[pl.BlockSpec((tm, s5w), lambda i: (i, 0)), pl.BlockSpec((tm, cw), lambda i: (i, 0)),
                  pl.BlockSpec((tm, d), lambda i: (i, 0)),
                  _resident(wglu, l), _resident(wout, l), _resident(g, l), _resident(b, l)],
        out_specs=pl.BlockSpec((tm, d), lambda i: (i, 0)),
        out_shape=jax.ShapeDtypeStruct((n, d), F32),
        compiler_params=_params(("arbitrary",)),
        name="mix_out",
    )(y5, ca, x, wglu, wout, g, b)


FFN_HALO = 16


def _ffn_kernel(*refs, alpha, tm, planes, nseq, blocks_per_seq):
    if planes:
        (x_ref, hg_c_ref, hv_c_ref, wg_ref, wv_ref, wd_ref, cwg_ref, cwv_ref, cbg_ref, cbv_ref, g_ref, b_ref,
         out_ref, tg_ref, tv_ref, xb_ref) = refs
    else:
        (x_ref, halo_ref, wg_ref, wv_ref, wd_ref, cwg_ref, cwv_ref, cbg_ref, cbv_ref, g_ref, b_ref,
         out_ref, tg_ref, tv_ref, xb_ref) = refs
    i = pl.program_id(0)
    j = pl.program_id(1)

    @pl.when(j == 0)
    def _():
        if planes:
            xb_ref[...] = x_ref[...].astype(BF16)
        else:
            start = (i % blocks_per_seq) == 0
            xb_ref[0:FFN_HALO, :] = jnp.where(start, 0.0, halo_ref[...]).astype(BF16)
            xb_ref[FFN_HALO:, :] = x_ref[...].astype(BF16)
        out_ref[...] = jnp.zeros_like(out_ref)

    xb = xb_ref[...]
    hg = jnp.dot(xb, wg_ref[...], preferred_element_type=F32)
    hv = jnp.dot(xb, wv_ref[...], preferred_element_type=F32)

    def conv3(h, hist_ref, cw_ref, cb_ref, tail_ref):
        w0, w1, w2 = cw_ref[0:1, :], cw_ref[1:2, :], cw_ref[2:3, :]
        if planes:
            hp = jnp.concatenate([hist_ref[...], h], axis=0)
            cur, prev1, prev2 = hp[2 * nseq:], hp[nseq:nseq + tm], hp[:tm]
            tail_ref[...] = h[tm - 2 * nseq:, :]
        else:
            cur = h[FFN_HALO:, :]
            prev1 = h[FFN_HALO - 1:FFN_HALO - 1 + tm, :]
            prev2 = h[FFN_HALO - 2:FFN_HALO - 2 + tm, :]
            tail_ref[...] = h[FFN_HALO + tm - SUBLANES:, :]
        return w2 * cur + w1 * prev1 + w0 * prev2 + cb_ref[...]

    cg = conv3(hg, hg_c_ref if planes else None, cwg_ref, cbg_ref, tg_ref)
    cv = conv3(hv, hv_c_ref if planes else None, cwv_ref, cbv_ref, tv_ref)
    act = (jax.nn.silu(cg) * cv).astype(BF16)
    out_ref[...] += jnp.dot(act, wd_ref[...], preferred_element_type=F32)

    @pl.when(j == pl.num_programs(1) - 1)
    def _():
        out_ref[...] = _ln(alpha * x_ref[...] + out_ref[...], g_ref[...], b_ref[...])


def _ffn(x, w_up, w_down, conv_w, conv_b, g, b, l, alpha, tm, tf, *, cache=None, t=None, nseq=None):
    n, d = x.shape
    dff = w_down.shape[1]
    nj = dff // tf
    fk = conv_w.shape[1]
    planes = cache is not None
    wspecs = [pl.BlockSpec((None, d, tf), lambda i, j: (l, 0, j)),
              pl.BlockSpec((None, d, tf), lambda i, j: (l, 0, nj + j)),
              pl.BlockSpec((None, tf, d), lambda i, j: (l, j, 0)),
              pl.BlockSpec((None, fk, tf), lambda i, j: (l, 0, j)),
              pl.BlockSpec((None, fk, tf), lambda i, j: (l, 0, nj + j)),
              pl.BlockSpec((None, 1, tf), lambda i, j: (l, 0, j)),
              pl.BlockSpec((None, 1, tf), lambda i, j: (l, 0, nj + j)),
              pl.BlockSpec((None, 1, d), lambda i, j: (l, 0, 0)), pl.BlockSpec((None, 1, d), lambda i, j: (l, 0, 0))]
    wargs = [w_up, w_up, w_down, conv_w, conv_w, conv_b, conv_b, g, b]
    if planes:
        assert n == tm
        hist = cache.shape[1]
        in_specs = [pl.BlockSpec((tm, d), lambda i, j: (i, 0)),
                    pl.BlockSpec((None, hist, tf), lambda i, j: (l, 0, j)),
                    pl.BlockSpec((None, hist, tf), lambda i, j: (l, 0, nj + j))] + wspecs
        args = [x, cache, cache] + wargs
        tail_rows, xb_rows, bps = hist, tm, 1
    else:
        hb = tm // FFN_HALO
        in_specs = [pl.BlockSpec((tm, d), lambda i, j: (i, 0), pipeline_mode=pl.Buffered(1)),
                    pl.BlockSpec((FFN_HALO, d), lambda i, j: (jnp.maximum(i * hb - 1, 0), 0))] + wspecs
        args = [x, x] + wargs
        tail_rows, xb_rows, bps = SUBLANES, tm + FFN_HALO, t // tm
    nblk = n // tm
    tail_spec = pl.BlockSpec((tail_rows, tf), lambda i, j: (i, j))
    return pl.pallas_call(
        functools.partial(_ffn_kernel, alpha=alpha, tm=tm, planes=planes, nseq=nseq, blocks_per_seq=bps),
        grid=(nblk, nj),
        in_specs=in_specs,
        out_specs=[pl.BlockSpec((tm, d), lambda i, j: (i, 0)), tail_spec, tail_spec],
        out_shape=[jax.ShapeDtypeStruct((n, d), F32),
                   jax.ShapeDtypeStruct((nblk * tail_rows, dff), F32),
                   jax.ShapeDtypeStruct((nblk * tail_rows, dff), F32)],
        scratch_shapes=[pltpu.VMEM((xb_rows, d), BF16)],
        compiler_params=_params(("arbitrary", "arbitrary")),
        name="ffn_planes" if planes else "ffn_seq",
    )(*args)


def _pe_kernel(x_ref, p_ref, wg_ref, wp_ref, g_ref, b_ref, out_ref, *, alpha):
    rs = x_ref.shape[0] // ROW_SPLIT
    for s in range(ROW_SPLIT):
        rows = slice(s * rs, (s + 1) * rs)
        x = x_ref[rows, :]
        gate = jax.nn.sigmoid(jnp.dot(x.astype(BF16), wg_ref[...], preferred_element_type=F32))
        e = jnp.dot(p_ref[rows, :].astype(BF16), wp_ref[...], preferred_element_type=F32)
        out_ref[rows, :] = _ln(alpha * x + gate * e, g_ref[...], b_ref[...])


def _pe(x, p, wg, wp, g, b, l, alpha, tm):
    n, d = x.shape
    pd = p.shape[2]
    return pl.pallas_call(
        functools.partial(_pe_kernel, alpha=alpha),
        grid=(n // tm,),
        in_specs=[pl.BlockSpec((tm, d), lambda i: (i, 0)), pl.BlockSpec((None, tm, pd), lambda i: (l, i, 0)),
                  _resident(wg, l), _resident(wp, l), _resident(g, l), _resident(b, l)],
        out_specs=pl.BlockSpec((tm, d), lambda i: (i, 0)),
        out_shape=jax.ShapeDtypeStruct((n, d), F32),
        compiler_params=_params(("arbitrary",)),
        name="pe_embed",
    )(x, p, wg, wp, g, b)


def _s5_discretise(lam_re, lam_im, log_dt, b_re, b_im):
    lr, li = lam_re.astype(F32), lam_im.astype(F32)
    dt = jnp.exp(log_dt.astype(F32))[..., None]
    mag = jnp.exp(lr * dt)
    ab_re = mag * jnp.cos(li * dt)
    ab_im = mag * jnp.sin(li * dt)
    num_re, num_im = ab_re - 1.0, ab_im
    den = lr * lr + li * li
    q_re = (num_re * lr + num_im * li) / den
    q_im = (num_im * lr - num_re * li) / den
    br, bi = b_re.astype(F32), b_im.astype(F32)
    bb_re = q_re[..., None] * br - q_im[..., None] * bi
    bb_im = q_re[..., None] * bi + q_im[..., None] * br
    return ab_re, ab_im, bb_re, bb_im


def _s5_block_weights(bb_re, bb_im, c_re, c_im):
    nl, g, p, h = bb_re.shape
    gs = g // SUPER
    eye = jnp.eye(gs, dtype=F32)

    def b_blocks(bb):
        t = bb.reshape(nl, SUPER, gs, p, h)
        return jnp.einsum('lsgph,gk->lsghkp', t, eye).reshape(nl, SUPER, gs * h, gs * p)

    def c_blocks(c):
        t = c.astype(F32).reshape(nl, SUPER, gs, h, p)
        return jnp.einsum('lsghp,gk->lsgpkh', t, eye).reshape(nl, SUPER, gs * p, gs * h)

    wb = jnp.concatenate([b_blocks(bb_re), b_blocks(bb_im)], axis=3).astype(BF16)
    wc = jnp.concatenate([c_blocks(c_re), -c_blocks(c_im)], axis=2).astype(BF16)
    return wb, wc


def _rows(v):
    return v.astype(F32).reshape(v.shape[0], 1, -1)


def kernel(x_prompt, x_sample, state_s5_re, state_s5_im, cache_conv, cache_ffn_conv, p_prompt, p_sample,
           w_in, s5_lam_re, s5_lam_im, s5_log_dt, s5_b_re, s5_b_im, s5_c_re, s5_c_im, s5_d, s5_w_glu,
           conv_w, conv_b, conv_ln_g, conv_ln_b, w_out, ln1_g, ln1_b,
           ffn_w_up, ffn_conv_w, ffn_conv_b, ffn_w_down, ln2_g, ln2_b,
           pe_w, pe_w_gate, ln3_g, ln3_b):
    nb, t, d = x_prompt.shape
    ns, ts, _ = x_sample.shape
    depth = w_in.shape[0]
    groups, nstate_g = s5_lam_re.shape[1], s5_lam_re.shape[2]
    s5w = groups * s5_d.shape[2]
    cw = d - s5w
    nstate = groups * nstate_g
    kk = conv_w.shape[1]
    fk = ffn_conv_w.shape[1]
    dff = ffn_w_down.shape[1]
    assert fk == 3 and ts >= fk - 1 and ts <= kk - 1 and t >= kk - 1
    alpha = (2.0 * depth) ** 0.25
    tm, tm_ffn, tf, tc, tconv = 512, 1024, 512, 256, 256

    xp = x_prompt.reshape(nb * t, d)
    xs = x_sample.transpose(1, 0, 2).reshape(ts * ns, d)
    zero_state = jnp.zeros((nb, nstate // LANES, LANES), F32)

    w_in_bf, wglu_bf, wout_bf = w_in.astype(BF16), s5_w_glu.astype(BF16), w_out.astype(BF16)
    wup_bf, wdown_bf = ffn_w_up.astype(BF16), ffn_w_down.astype(BF16)
    wpe_bf, wgate_bf = pe_w.astype(BF16), pe_w_gate.astype(BF16)
    ab_re, ab_im, bb_re, bb_im = _s5_discretise(s5_lam_re, s5_lam_im, s5_log_dt, s5_b_re, s5_b_im)
    wb, wc = _s5_block_weights(bb_re, bb_im, s5_c_re, s5_c_im)
    a_tile_re, a_tile_im = (v.reshape(depth, nstate // LANES, LANES) for v in (ab_re, ab_im))
    a_row_re, a_row_im = (v.reshape(depth, 1, nstate) for v in (ab_re, ab_im))
    d_rows = _rows(s5_d.reshape(depth, s5w))
    conv_w1 = conv_w.astype(F32)
    conv_w8 = jnp.repeat(conv_w1, SUBLANES, axis=1)
    conv_b_rows, cg_rows, cb_rows = _rows(conv_b), _rows(conv_ln_g), _rows(conv_ln_b)
    fcw, fcb = ffn_conv_w.astype(F32), _rows(ffn_conv_b)
    l1g, l1b, l2g, l2b, l3g, l3b = (_rows(v) for v in (ln1_g, ln1_b, ln2_g, ln2_b, ln3_g, ln3_b))
    pp = p_prompt.reshape(depth, nb * t, -1)
    ps = p_sample.transpose(0, 2, 1, 3).reshape(depth, ts * ns, -1)
    h0_re = state_s5_re.reshape(depth, ns, nstate).astype(F32)
    h0_im = state_s5_im.reshape(depth, ns, nstate).astype(F32)
    cache_t = cache_conv.transpose(0, 2, 1, 3)
    fcache = cache_ffn_conv.transpose(0, 2, 1, 3).reshape(depth, (fk - 1) * ns, 2 * dff)

    p_re, p_im, p_cv, p_ff = [], [], [], []
    s_re, s_im, s_cv, s_ff = [], [], [], []
    for l in range(depth):
        u, c = _in_proj(xp, w_in_bf, l, s5w, cw, tm)
        y5, hre, him = _s5_seq(u, wb, wc, a_tile_re, a_tile_im, d_rows, zero_state, zero_state, l, nb, t, tc)
        ca = _conv_seq(c, conv_w1, conv_b_rows, cg_rows, cb_rows, l, t, tconv, kk)
        x1 = _mix_out(y5, ca, xp, wglu_bf, wout_bf, l1g, l1b, l, alpha, tm)
        x2, tg, tv = _ffn(x1, wup_bf, wdown_bf, fcw, fcb, l2g, l2b, l, alpha, tm_ffn, tf, t=t)
        xp = _pe(x2, pp, wgate_bf, wpe_bf, l3g, l3b, l, alpha, tm)
        p_re.append(hre.reshape(nb, groups, nstate_g))
        p_im.append(him.reshape(nb, groups, nstate_g))
        p_cv.append(jnp.stack([c[(b + 1) * t - (kk - 1):(b + 1) * t] for b in range(nb)]))
        bps = t // tm_ffn
        tails = jnp.concatenate([tg, tv], axis=1).reshape(nb, bps, SUBLANES, 2 * dff)
        p_ff.append(tails[:, bps - 1, SUBLANES - (fk - 1):])

        u, c = _in_proj(xs, w_in_bf, l, s5w, cw, ts * ns)
        y5, hre, him = _s5_planes(u, wb, wc, a_row_re, a_row_im, d_rows, h0_re, h0_im, l, ts, ns)
        c_t = c.reshape(ts, ns, cw)
        ca = _conv_planes(cache_t, c_t, conv_w8, conv_b_rows, cg_rows, cb_rows, l, kk, 32).reshape(ts * ns, cw)
        x1 = _mix_out(y5, ca, xs, wglu_bf, wout_bf, l1g, l1b, l, alpha, ts * ns)
        x2, tg, tv = _ffn(x1, wup_bf, wdown_bf, fcw, fcb, l2g, l2b, l, alpha, ts * ns, tf, cache=fcache, nseq=ns)
        xs = _pe(x2, ps, wgate_bf, wpe_bf, l3g, l3b, l, alpha, ts * ns)
        s_re.append(hre.reshape(ns, groups, nstate_g))
        s_im.append(him.reshape(ns, groups, nstate_g))
        s_cv.append(jnp.concatenate([cache_conv[l][:, ts:], c_t.transpose(1, 0, 2)], axis=1))
        s_ff.append(jnp.concatenate([tg, tv], axis=1).reshape(fk - 1, ns, 2 * dff).transpose(1, 0, 2))

    y_prompt = xp.reshape(nb, t, d)
    y_sample = xs.reshape(ts, ns, d).transpose(1, 0, 2)
    return (y_prompt, y_sample,
            jnp.stack(p_re), jnp.stack(p_im), jnp.stack(p_cv), jnp.stack(p_ff),
            jnp.stack(s_re), jnp.stack(s_im), jnp.stack(s_cv), jnp.stack(s_ff))
```

```python
import functools
import math

import jax
import jax.numpy as jnp
from jax import lax
from jax.experimental import pallas as pl
from jax.experimental.pallas import tpu as pltpu

F32 = jnp.float32
BF16 = jnp.bfloat16

LN_EPS = 1e-5
V7X_VMEM_LIMIT_BYTES = 56 * 1024 * 1024
SUBLANES = 8
LANES = 128
S5_GROUP_CH = 16
S5_STATE = 64
SUPER = 4
ROW_SPLIT = 2


def _params(sem):
    return pltpu.CompilerParams(dimension_semantics=sem, vmem_limit_bytes=V7X_VMEM_LIMIT_BYTES)


def _resident(stacked, l):
    shape = stacked.shape[1:]
    nd = len(shape)
    return pl.BlockSpec((None,) + shape, lambda *_: (l,) + (0,) * nd, pipeline_mode=pl.Buffered(1))


def _ln(r, g, b):
    mu = jnp.mean(r, axis=-1, keepdims=True)
    d = r - mu
    var = jnp.mean(d * d, axis=-1, keepdims=True)
    return d * lax.rsqrt(var + LN_EPS) * g + b


def _in_proj_kernel(x_ref, w_ref, u_ref, c_ref, *, s5w, cw):
    rs = x_ref.shape[0] // ROW_SPLIT
    for s in range(ROW_SPLIT):
        rows = slice(s * rs, (s + 1) * rs)
        xb = x_ref[rows, :].astype(BF16)
        u_ref[rows, :] = jnp.dot(xb, w_ref[:, :s5w], preferred_element_type=F32)
        cv = jnp.dot(xb, w_ref[:, s5w:s5w + cw], preferred_element_type=F32)
        cg = jnp.dot(xb, w_ref[:, s5w + cw:], preferred_element_type=F32)
        c_ref[rows, :] = cv * jax.nn.sigmoid(cg)


def _in_proj(x, w_in_bf, l, s5w, cw, tm):
    n, d = x.shape
    return pl.pallas_call(
        functools.partial(_in_proj_kernel, s5w=s5w, cw=cw),
        grid=(n // tm,),
        in_specs=[pl.BlockSpec((tm, d), lambda i: (i, 0)), _resident(w_in_bf, l)],
        out_specs=[pl.BlockSpec((tm, s5w), lambda i: (i, 0)), pl.BlockSpec((tm, cw), lambda i: (i, 0))],
        out_shape=[jax.ShapeDtypeStruct((n, s5w), F32), jax.ShapeDtypeStruct((n, cw), F32)],
        compiler_params=_params(("arbitrary",)),
        name="in_proj",
    )(x, w_in_bf)


def _s5_seq_kernel(u_ref, wb_ref, wc_ref, are_ref, aim_ref, d_ref, h0re_ref, h0im_ref,
                   y_ref, hre_ref, him_ref, sre, sim, *, tc, sgw):
    step_rows = SUBLANES
    k = pl.program_id(1)

    @pl.when(k == 0)
    def _():
        hre_ref[...] = h0re_ref[...]
        him_ref[...] = h0im_ref[...]

    u = u_ref[...]
    ub = u.astype(BF16)
    for v in range(SUPER):
        r = jnp.dot(ub[:, v * sgw:(v + 1) * sgw], wb_ref[v], preferred_element_type=F32)
        for i in range(SUBLANES):
            sre[v, pl.ds(i, tc, stride=step_rows), :] = r[:, i * LANES:(i + 1) * LANES]
            sim[v, pl.ds(i, tc, stride=step_rows), :] = r[:, (SUBLANES + i) * LANES:(SUBLANES + i + 1) * LANES]

    a_re = [are_ref[v * SUBLANES:(v + 1) * SUBLANES, :] for v in range(SUPER)]
    a_im = [aim_ref[v * SUBLANES:(v + 1) * SUBLANES, :] for v in range(SUPER)]
    init = tuple(hre_ref[0, v * SUBLANES:(v + 1) * SUBLANES, :] for v in range(SUPER)) + \
        tuple(him_ref[0, v * SUBLANES:(v + 1) * SUBLANES, :] for v in range(SUPER))

    def step(t, carry):
        row = pl.multiple_of(t * step_rows, step_rows)
        new_re, new_im = [], []
        for v in range(SUPER):
            hr, hi = carry[v], carry[SUPER + v]
            nr = a_re[v] * hr - a_im[v] * hi + sre[v, pl.ds(row, step_rows), :]
            ni = a_re[v] * hi + a_im[v] * hr + sim[v, pl.ds(row, step_rows), :]
            sre[v, pl.ds(row, step_rows), :] = nr
            sim[v, pl.ds(row, step_rows), :] = ni
            new_re.append(nr)
            new_im.append(ni)
        return tuple(new_re) + tuple(new_im)

    fin = lax.fori_loop(0, tc, step, init, unroll=8)
    for v in range(SUPER):
        hre_ref[0, v * SUBLANES:(v + 1) * SUBLANES, :] = fin[v]
        him_ref[0, v * SUBLANES:(v + 1) * SUBLANES, :] = fin[SUPER + v]

    for v in range(SUPER):
        cols = [sre[v, pl.ds(i, tc, stride=step_rows), :] for i in range(SUBLANES)]
        cols += [sim[v, pl.ds(i, tc, stride=step_rows), :] for i in range(SUBLANES)]
        hcat = jnp.concatenate(cols, axis=1).astype(BF16)
        y = jnp.dot(hcat, wc_ref[v], preferred_element_type=F32)
        y_ref[:, v * sgw:(v + 1) * sgw] = y + d_ref[:, v * sgw:(v + 1) * sgw] * u[:, v * sgw:(v + 1) * sgw]


def _s5_seq(u, wb, wc, a_re, a_im, d, h0re, h0im, l, nb, t, tc):
    s5w = u.shape[1]
    sgw = s5w // SUPER
    nk = t // tc
    rows = a_re.shape[1]
    st_spec = pl.BlockSpec((1, rows, LANES), lambda b, k: (b, 0, 0))
    return pl.pallas_call(
        functools.partial(_s5_seq_kernel, tc=tc, sgw=sgw),
        grid=(nb, nk),
        in_specs=[pl.BlockSpec((tc, s5w), lambda b, k: (b * nk + k, 0)),
                  _resident(wb, l), _resident(wc, l), _resident(a_re, l), _resident(a_im, l),
                  _resident(d, l), st_spec, st_spec],
        out_specs=[pl.BlockSpec((tc, s5w), lambda b, k: (b * nk + k, 0)), st_spec, st_spec],
        out_shape=[jax.ShapeDtypeStruct((nb * t, s5w), F32),
                   jax.ShapeDtypeStruct((nb, rows, LANES), F32), jax.ShapeDtypeStruct((nb, rows, LANES), F32)],
        scratch_shapes=[pltpu.VMEM((SUPER, tc * SUBLANES, LANES), F32),
                        pltpu.VMEM((SUPER, tc * SUBLANES, LANES), F32)],
        compiler_params=_params(("arbitrary", "arbitrary")),
        name="s5_seq",
    )(u, wb, wc, a_re, a_im, d, h0re, h0im)


def _s5_planes_kernel(u_ref, wb_ref, wc_ref, are_ref, aim_ref, d_ref, h0re_ref, h0im_ref,
                      y_ref, hre_ref, him_ref, *, nt, nseq, half):
    u = u_ref[...]
    r = jnp.dot(u.astype(BF16), wb_ref[0], preferred_element_type=F32)
    a_re, a_im = are_ref[...], aim_ref[...]
    hr, hi = h0re_ref[...], h0im_ref[...]
    hs_re, hs_im = [], []
    for t in range(nt):
        br = r[t * nseq:(t + 1) * nseq, :half]
        bi = r[t * nseq:(t + 1) * nseq, half:]
        hr, hi = a_re * hr - a_im * hi + br, a_re * hi + a_im * hr + bi
        hs_re.append(hr)
        hs_im.append(hi)
    hre_ref[...] = hr
    him_ref[...] = hi
    hcat = jnp.concatenate([jnp.concatenate(hs_re, axis=0), jnp.concatenate(hs_im, axis=0)], axis=1)
    y = jnp.dot(hcat.astype(BF16), wc_ref[0], preferred_element_type=F32)
    y_ref[...] = y + d_ref[...] * u


def _s5_planes(u, wb, wc, a_re, a_im, d, h0re, h0im, l, nt, nseq):
    n, s5w = u.shape
    sgw = s5w // SUPER
    nstate = h0re.shape[2]
    half = nstate // SUPER
    return pl.pallas_call(
        functools.partial(_s5_planes_kernel, nt=nt, nseq=nseq, half=half),
        grid=(SUPER,),
        in_specs=[pl.BlockSpec((n, sgw), lambda v: (0, v)),
                  pl.BlockSpec((None, 1) + wb.shape[2:], lambda v: (l, v, 0, 0)),
                  pl.BlockSpec((None, 1) + wc.shape[2:], lambda v: (l, v, 0, 0)),
                  pl.BlockSpec((None, 1, half), lambda v: (l, 0, v)),
                  pl.BlockSpec((None, 1, half), lambda v: (l, 0, v)),
                  pl.BlockSpec((None, 1, sgw), lambda v: (l, 0, v)),
                  pl.BlockSpec((None, nseq, half), lambda v: (l, 0, v)),
                  pl.BlockSpec((None, nseq, half), lambda v: (l, 0, v))],
        out_specs=[pl.BlockSpec((n, sgw), lambda v: (0, v)),
                   pl.BlockSpec((nseq, half), lambda v: (0, v)), pl.BlockSpec((nseq, half), lambda v: (0, v))],
        out_shape=[jax.ShapeDtypeStruct((n, s5w), F32),
                   jax.ShapeDtypeStruct((nseq, nstate), F32), jax.ShapeDtypeStruct((nseq, nstate), F32)],
        compiler_params=_params(("arbitrary",)),
        name="s5_planes",
    )(u, wb, wc, a_re, a_im, d, h0re, h0im)


CONV_ROW_TILE = 32
CONV_COL_TILE = 512


def _conv_seq_kernel(c_ref, halo_ref, w_ref, b_ref, g_ref, bt_ref, out_ref, xp_ref, sh_ref, y_ref,
                     *, tm, kk, halo, blocks_per_seq):
    i = pl.program_id(0)
    start = (i % blocks_per_seq) == 0
    xp_ref[0:halo, :] = jnp.where(start, 0.0, halo_ref[...])
    xp_ref[halo:halo + tm, :] = c_ref[...]
    ncopy = tm + halo - SUBLANES
    for b in range(1, SUBLANES):
        sh_ref[b - 1, :, :] = xp_ref[pl.ds(b, ncopy), :]
    rt = CONV_ROW_TILE
    for lc in range(c_ref.shape[1] // CONV_COL_TILE):
        cols = slice(lc * CONV_COL_TILE, (lc + 1) * CONV_COL_TILE)

        def tile(r, carry, cols=cols):
            r0 = pl.multiple_of(r * rt, rt)
            acc = jnp.broadcast_to(b_ref[:, cols], (rt, CONV_COL_TILE))
            for j in range(kk):
                a, b = divmod(halo - j, SUBLANES)
                k = kk - 1 - j
                rows = pl.ds(r0 + SUBLANES * a, rt)
                xs = xp_ref[rows, cols] if b == 0 else sh_ref[b - 1, rows, cols]
                acc = acc + w_ref[k:k + 1, cols] * xs
            y_ref[pl.ds(r0, rt), cols] = acc
            return carry

        lax.fori_loop(0, tm // rt, tile, 0)
    out_ref[...] = jax.nn.silu(_ln(y_ref[...], g_ref[...], bt_ref[...]))


def _conv_seq(c, w, b, g, bt, l, t, tm, kk):
    n, cw = c.shape
    halo = 32
    assert kk - 1 <= halo and tm % halo == 0 and t % tm == 0
    hb = tm // halo
    return pl.pallas_call(
        functools.partial(_conv_seq_kernel, tm=tm, kk=kk, halo=halo, blocks_per_seq=t // tm),
        grid=(n // tm,),
        in_specs=[pl.BlockSpec((tm, cw), lambda i: (i, 0)),
                  pl.BlockSpec((halo, cw), lambda i: (jnp.maximum(i * hb - 1, 0), 0)),
                  _resident(w, l), _resident(b, l), _resident(g, l), _resident(bt, l)],
        out_specs=pl.BlockSpec((tm, cw), lambda i: (i, 0)),
        out_shape=jax.ShapeDtypeStruct((n, cw), F32),
        scratch_shapes=[pltpu.VMEM((tm + halo, cw), F32),
                        pltpu.VMEM((SUBLANES - 1, tm + halo - SUBLANES, cw), F32),
                        pltpu.VMEM((tm, cw), F32)],
        compiler_params=_params(("arbitrary",)),
        name="conv_seq",
    )(c, c, w, b, g, bt)


def _conv_planes_kernel(cache_ref, c_ref, w_ref, b_ref, g_ref, bt_ref, out_ref, *, nt, kk):
    rows, cw = c_ref.shape[1], c_ref.shape[2]
    gam, bet = g_ref[...], bt_ref[...]
    for t in range(nt):
        acc = jnp.broadcast_to(b_ref[...], (rows, cw))
        for k in range(kk):
            p = t + k
            xs = cache_ref[p] if p < kk - 1 else c_ref[p - (kk - 1)]
            w8 = w_ref[SUBLANES * k:SUBLANES * (k + 1), :]
            acc = acc + jnp.concatenate([w8] * (rows // SUBLANES), axis=0) * xs
        out_ref[t] = jax.nn.silu(_ln(acc, gam, bet))


def _conv_planes(cache_t, c_t, w8, b, g, bt, l, kk, rows):
    nt, nseq, cw = c_t.shape
    return pl.pallas_call(
        functools.partial(_conv_planes_kernel, nt=nt, kk=kk),
        grid=(nseq // rows,),
        in_specs=[pl.BlockSpec((None, kk - 1, rows, cw), lambda i: (l, 0, i, 0)),
                  pl.BlockSpec((nt, rows, cw), lambda i: (0, i, 0)),
                  _resident(w8, l), _resident(b, l), _resident(g, l), _resident(bt, l)],
        out_specs=pl.BlockSpec((nt, rows, cw), lambda i: (0, i, 0)),
        out_shape=jax.ShapeDtypeStruct((nt, nseq, cw), F32),
        compiler_params=_params(("arbitrary",)),
        name="conv_planes",
    )(cache_t, c_t, w8, b, g, bt)


def _mix_out_kernel(y5_ref, ca_ref, x_ref, wglu_ref, wout_ref, g_ref, b_ref, out_ref, *, alpha, s5w):
    rs = x_ref.shape[0] // ROW_SPLIT
    for s in range(ROW_SPLIT):
        rows = slice(s * rs, (s + 1) * rs)
        g5 = jax.nn.gelu(y5_ref[rows, :])
        gate = jnp.dot(g5.astype(BF16), wglu_ref[...], preferred_element_type=F32)
        s5o = g5 * jax.nn.sigmoid(gate)
        mix = jnp.dot(s5o.astype(BF16), wout_ref[:s5w, :], preferred_element_type=F32)
        mix = mix + jnp.dot(ca_ref[rows, :].astype(BF16), wout_ref[s5w:, :], preferred_element_type=F32)
        out_ref[rows, :] = _ln(alpha * x_ref[rows, :] + mix, g_ref[...], b_ref[...])


def _mix_out(y5, ca, x, wglu, wout, g, b, l, alpha, tm):
    n, d = x.shape
    s5w, cw = y5.shape[1], ca.shape[1]
    return pl.pallas_call(
        functools.partial(_mix_out_kernel, alpha=alpha, s5w=s5w),
        grid=(n // tm,),
        in_specs=[pl.BlockSpec((tm, s5w), lambda i: (i, 0)), pl.BlockSpec((tm, cw), lambda i: (i, 0)),
                  pl.BlockSpec((tm, d), lambda i: (i, 0)),
                  _resident(wglu, l), _resident(wout, l), _resident(g, l), _resident(b, l)],
        out_specs=pl.BlockSpec((tm, d), lambda i: (i, 0)),
        out_shape=jax.ShapeDtypeStruct((n, d), F32),
        compiler_params=_params(("arbitrary",)),
        name="mix_out",
    )(y5, ca, x, wglu, wout, g, b)


FFN_HALO = 16


def _ffn_kernel(*refs, alpha, tm, planes, nseq, blocks_per_seq):
    if planes:
        (x_ref, hg_c_ref, hv_c_ref, wg_ref, wv_ref, wd_ref, cwg_ref, cwv_ref, cbg_ref, cbv_ref, g_ref, b_ref,
         out_ref, tg_ref, tv_ref, xb_ref) = refs
    else:
        (x_ref, halo_ref, wg_ref, wv_ref, wd_ref, cwg_ref, cwv_ref, cbg_ref, cbv_ref, g_ref, b_ref,
         out_ref, tg_ref, tv_ref, xb_ref) = refs
    i = pl.program_id(0)
    j = pl.program_id(1)

    @pl.when(j == 0)
    def _():
        if planes:
            xb_ref[...] = x_ref[...].astype(BF16)
        else:
            start = (i % blocks_per_seq) == 0
            xb_ref[0:FFN_HALO, :] = jnp.where(start, 0.0, halo_ref[...]).astype(BF16)
            xb_ref[FFN_HALO:, :] = x_ref[...].astype(BF16)
        out_ref[...] = jnp.zeros_like(out_ref)

    xb = xb_ref[...]
    hg = jnp.dot(xb, wg_ref[...], preferred_element_type=F32)
    hv = jnp.dot(xb, wv_ref[...], preferred_element_type=F32)

    def conv3(h, hist_ref, cw_ref, cb_ref, tail_ref):
        w0, w1, w2 = cw_ref[0:1, :], cw_ref[1:2, :], cw_ref[2:3, :]
        if planes:
            hp = jnp.concatenate([hist_ref[...], h], axis=0)
            cur, prev1, prev2 = hp[2 * nseq:], hp[nseq:nseq + tm], hp[:tm]
            tail_ref[...] = h[tm - 2 * nseq:, :]
        else:
            cur = h[FFN_HALO:, :]
            prev1 = h[FFN_HALO - 1:FFN_HALO - 1 + tm, :]
            prev2 = h[FFN_HALO - 2:FFN_HALO - 2 + tm, :]
            tail_ref[...] = h[FFN_HALO + tm - SUBLANES:, :]
        return w2 * cur + w1 * prev1 + w0 * prev2 + cb_ref[...]

    cg = conv3(hg, hg_c_ref if planes else None, cwg_ref, cbg_ref, tg_ref)
    cv = conv3(hv, hv_c_ref if planes else None, cwv_ref, cbv_ref, tv_ref)
    act = (jax.nn.silu(cg) * cv).astype(BF16)
    out_ref[...] += jnp.dot(act, wd_ref[...], preferred_element_type=F32)

    @pl.when(j == pl.num_programs(1) - 1)
    def _():
        out_ref[...] = _ln(alpha * x_ref[...] + out_ref[...], g_ref[...], b_ref[...])


def _ffn(x, w_up, w_down, conv_w, conv_b, g, b, l, alpha, tm, tf, *, cache=None, t=None, nseq=None):
    n, d = x.shape
    dff = w_down.shape[1]
    nj = dff // tf
    fk = conv_w.shape[1]
    planes = cache is not None
    wspecs = [pl.BlockSpec((None, d, tf), lambda i, j: (l, 0, j)),
              pl.BlockSpec((None, d, tf), lambda i, j: (l, 0, nj + j)),
              pl.BlockSpec((None, tf, d), lambda i, j: (l, j, 0)),
              pl.BlockSpec((None, fk, tf), lambda i, j: (l, 0, j)),
              pl.BlockSpec((None, fk, tf), lambda i, j: (l, 0, nj + j)),
              pl.BlockSpec((None, 1, tf), lambda i, j: (l, 0, j)),
              pl.BlockSpec((None, 1, tf), lambda i, j: (l, 0, nj + j)),
              pl.BlockSpec((None, 1, d), lambda i, j: (l, 0, 0)), pl.BlockSpec((None, 1, d), lambda i, j: (l, 0, 0))]
    wargs = [w_up, w_up, w_down, conv_w, conv_w, conv_b, conv_b, g, b]
    if planes:
        assert n == tm
        hist = cache.shape[1]
        in_specs = [pl.BlockSpec((tm, d), lambda i, j: (i, 0)),
                    pl.BlockSpec((None, hist, tf), lambda i, j: (l, 0, j)),
                    pl.BlockSpec((None, hist, tf), lambda i, j: (l, 0, nj + j))] + wspecs
        args = [x, cache, cache] + wargs
        tail_rows, xb_rows, bps = hist, tm, 1
    else:
        hb = tm // FFN_HALO
        in_specs = [pl.BlockSpec((tm, d), lambda i, j: (i, 0), pipeline_mode=pl.Buffered(1)),
                    pl.BlockSpec((FFN_HALO, d), lambda i, j: (jnp.maximum(i * hb - 1, 0), 0))] + wspecs
        args = [x, x] + wargs
        tail_rows, xb_rows, bps = SUBLANES, tm + FFN_HALO, t // tm
    nblk = n // tm
    tail_spec = pl.BlockSpec((tail_rows, tf), lambda i, j: (i, j))
    return pl.pallas_call(
        functools.partial(_ffn_kernel, alpha=alpha, tm=tm, planes=planes, nseq=nseq, blocks_per_seq=bps),
        grid=(nblk, nj),
        in_specs=in_specs,
        out_specs=[pl.BlockSpec((tm, d), lambda i, j: (i, 0)), tail_spec, tail_spec],
        out_shape=[jax.ShapeDtypeStruct((n, d), F32),
                   jax.ShapeDtypeStruct((nblk * tail_rows, dff), F32),
                   jax.ShapeDtypeStruct((nblk * tail_rows, dff), F32)],
        scratch_shapes=[pltpu.VMEM((xb_rows, d), BF16)],
        compiler_params=_params(("arbitrary", "arbitrary")),
        name="ffn_planes" if planes else "ffn_seq",
    )(*args)


def _pe_kernel(x_ref, p_ref, wg_ref, wp_ref, g_ref, b_ref, out_ref, *, alpha):
    rs = x_ref.shape[0] // ROW_SPLIT
    for s in range(ROW_SPLIT):
        rows = slice(s * rs, (s + 1) * rs)
        x = x_ref[rows, :]
        gate = jax.nn.sigmoid(jnp.dot(x.astype(BF16), wg_ref[...], preferred_element_type=F32))
        e = jnp.dot(p_ref[rows, :].astype(BF16), wp_ref[...], preferred_element_type=F32)
        out_ref[rows, :] = _ln(alpha * x + gate * e, g_ref[...], b_ref[...])


def _pe(x, p, wg, wp, g, b, l, alpha, tm):
    n, d = x.shape
    pd = p.shape[2]
    return pl.pallas_call(
        functools.partial(_pe_kernel, alpha=alpha),
        grid=(n // tm,),
        in_specs=[pl.BlockSpec((tm, d), lambda i: (i, 0)), pl.BlockSpec((None, tm, pd), lambda i: (l, i, 0)),
                  _resident(wg, l), _resident(wp, l), _resident(g, l), _resident(b, l)],
        out_specs=pl.BlockSpec((tm, d), lambda i: (i, 0)),
        out_shape=jax.ShapeDtypeStruct((n, d), F32),
        compiler_params=_params(("arbitrary",)),
        name="pe_embed",
    )(x, p, wg, wp, g, b)


def _s5_discretise(lam_re, lam_im, log_dt, b_re, b_im):
    lr, li = lam_re.astype(F32), lam_im.astype(F32)
    dt = jnp.exp(log_dt.astype(F32))[..., None]
    mag = jnp.exp(lr * dt)
    ab_re = mag * jnp.cos(li * dt)
    ab_im = mag * jnp.sin(li * dt)
    num_re, num_im = ab_re - 1.0, ab_im
    den = lr * lr + li * li
    q_re = (num_re * lr + num_im * li) / den
    q_im = (num_im * lr - num_re * li) / den
    br, bi = b_re.astype(F32), b_im.astype(F32)
    bb_re = q_re[..., None] * br - q_im[..., None] * bi
    bb_im = q_re[..., None] * bi + q_im[..., None] * br
    return ab_re, ab_im, bb_re, bb_im


def _s5_block_weights(bb_re, bb_im, c_re, c_im):
    nl, g, p, h = bb_re.shape
    gs = g // SUPER
    on_diag = jnp.eye(gs, dtype=bool)[None, None, :, None, :, None]

    def b_blocks(bb):
        t = bb.astype(BF16).reshape(nl, SUPER, gs, p, h).transpose(0, 1, 2, 4, 3)
        w = jnp.where(on_diag, t[:, :, :, :, None, :], 0)
        return w.reshape(nl, SUPER, gs * h, gs * p)

    def c_blocks(c):
        t = c.astype(BF16).reshape(nl, SUPER, gs, h, p).transpose(0, 1, 2, 4, 3)
        w = jnp.where(on_diag, t[:, :, :, :, None, :], 0)
        return w.reshape(nl, SUPER, gs * p, gs * h)

    wb = jnp.concatenate([b_blocks(bb_re), b_blocks(bb_im)], axis=3)
    wc = jnp.concatenate([c_blocks(c_re), c_blocks(-c_im)], axis=2)
    return wb, wc


def _rows(v):
    return v.astype(F32).reshape(v.shape[0], 1, -1)


def kernel(x_prompt, x_sample, state_s5_re, state_s5_im, cache_conv, cache_ffn_conv, p_prompt, p_sample,
           w_in, s5_lam_re, s5_lam_im, s5_log_dt, s5_b_re, s5_b_im, s5_c_re, s5_c_im, s5_d, s5_w_glu,
           conv_w, conv_b, conv_ln_g, conv_ln_b, w_out, ln1_g, ln1_b,
           ffn_w_up, ffn_conv_w, ffn_conv_b, ffn_w_down, ln2_g, ln2_b,
           pe_w, pe_w_gate, ln3_g, ln3_b):
    nb, t, d = x_prompt.shape
    ns, ts, _ = x_sample.shape
    depth = w_in.shape[0]
    groups, nstate_g = s5_lam_re.shape[1], s5_lam_re.shape[2]
    s5w = groups * s5_d.shape[2]
    cw = d - s5w
    nstate = groups * nstate_g
    kk = conv_w.shape[1]
    fk = ffn_conv_w.shape[1]
    dff = ffn_w_down.shape[1]
    assert fk == 3 and ts >= fk - 1 and ts <= kk - 1 and t >= kk - 1
    alpha = (2.0 * depth) ** 0.25
    tm, tm_ffn, tf, tc, tconv = 512, 1024, 512, 512, 256

    xp = x_prompt.reshape(nb * t, d)
    xs = x_sample.transpose(1, 0, 2).reshape(ts * ns, d)
    zero_state = jnp.zeros((nb, nstate // LANES, LANES), F32)

    w_in_bf, wglu_bf, wout_bf = w_in.astype(BF16), s5_w_glu.astype(BF16), w_out.astype(BF16)
    wup_bf, wdown_bf = ffn_w_up.astype(BF16), ffn_w_down.astype(BF16)
    wpe_bf, wgate_bf = pe_w.astype(BF16), pe_w_gate.astype(BF16)
    ab_re, ab_im, bb_re, bb_im = _s5_discretise(s5_lam_re, s5_lam_im, s5_log_dt, s5_b_re, s5_b_im)
    wb, wc = _s5_block_weights(bb_re, bb_im, s5_c_re, s5_c_im)
    a_tile_re, a_tile_im = (v.reshape(depth, nstate // LANES, LANES) for v in (ab_re, ab_im))
    a_row_re, a_row_im = (v.reshape(depth, 1, nstate) for v in (ab_re, ab_im))
    d_rows = _rows(s5_d.reshape(depth, s5w))
    conv_w1 = conv_w.astype(F32)
    conv_w8 = jnp.repeat(conv_w1, SUBLANES, axis=1)
    conv_b_rows, cg_rows, cb_rows = _rows(conv_b), _rows(conv_ln_g), _rows(conv_ln_b)
    fcw, fcb = ffn_conv_w.astype(F32), _rows(ffn_conv_b)
    l1g, l1b, l2g, l2b, l3g, l3b = (_rows(v) for v in (ln1_g, ln1_b, ln2_g, ln2_b, ln3_g, ln3_b))
    pp = p_prompt.reshape(depth, nb * t, -1)
    ps = p_sample.transpose(0, 2, 1, 3).reshape(depth, ts * ns, -1)
    h0_re = state_s5_re.reshape(depth, ns, nstate).astype(F32)
    h0_im = state_s5_im.reshape(depth, ns, nstate).astype(F32)
    cache_t = cache_conv.transpose(0, 2, 1, 3)
    fcache = cache_ffn_conv.transpose(0, 2, 1, 3).reshape(depth, (fk - 1) * ns, 2 * dff)

    p_re, p_im, p_cv, p_tg, p_tv = [], [], [], [], []
    s_re, s_im, s_c, s_tg, s_tv = [], [], [], [], []
    for l in range(depth):
        u, c = _in_proj(xp, w_in_bf, l, s5w, cw, tm)
        y5, hre, him = _s5_seq(u, wb, wc, a_tile_re, a_tile_im, d_rows, zero_state, zero_state, l, nb, t, tc)
        ca = _conv_seq(c, conv_w1, conv_b_rows, cg_rows, cb_rows, l, t, tconv, kk)
        x1 = _mix_out(y5, ca, xp, wglu_bf, wout_bf, l1g, l1b, l, alpha, tm)
        x2, tg, tv = _ffn(x1, wup_bf, wdown_bf, fcw, fcb, l2g, l2b, l, alpha, tm_ffn, tf, t=t)
        xp = _pe(x2, pp, wgate_bf, wpe_bf, l3g, l3b, l, alpha, tm)
        p_re.append(hre)
        p_im.append(him)
        p_cv.append(jnp.stack([c[(b + 1) * t - (kk - 1):(b + 1) * t] for b in range(nb)]))
        p_tg.append(tg)
        p_tv.append(tv)

        u, c = _in_proj(xs, w_in_bf, l, s5w, cw, ts * ns)
        y5, hre, him = _s5_planes(u, wb, wc, a_row_re, a_row_im, d_rows, h0_re, h0_im, l, ts, ns)
        ca = _conv_planes(cache_t, c.reshape(ts, ns, cw), conv_w8, conv_b_rows, cg_rows, cb_rows, l, kk, 32)
        x1 = _mix_out(y5, ca.reshape(ts * ns, cw), xs, wglu_bf, wout_bf, l1g, l1b, l, alpha, ts * ns)
        x2, tg, tv = _ffn(x1, wup_bf, wdown_bf, fcw, fcb, l2g, l2b, l, alpha, ts * ns, tf, cache=fcache, nseq=ns)
        xs = _pe(x2, ps, wgate_bf, wpe_bf, l3g, l3b, l, alpha, ts * ns)
        s_re.append(hre)
        s_im.append(him)
        s_c.append(c)
        s_tg.append(tg)
        s_tv.append(tv)

    y_prompt = xp.reshape(nb, t, d)
    y_sample = xs.reshape(ts, ns, d).transpose(1, 0, 2)
    bps = t // tm_ffn
    p_tails = jnp.concatenate([jnp.stack(p_tg), jnp.stack(p_tv)], axis=2).reshape(depth, nb, bps, SUBLANES, 2 * dff)
    p_ff = p_tails[:, :, bps - 1, SUBLANES - (fk - 1):]
    s_cv = jnp.concatenate([cache_conv[:, :, ts:], jnp.stack(s_c).reshape(depth, ts, ns, cw).transpose(0, 2, 1, 3)],
                           axis=2)
    s_ff = jnp.concatenate([jnp.stack(s_tg), jnp.stack(s_tv)], axis=2)
    s_ff = s_ff.reshape(depth, fk - 1, ns, 2 * dff).transpose(0, 2, 1, 3)
    return (y_prompt, y_sample,
            jnp.stack(p_re).reshape(depth, nb, groups, nstate_g), jnp.stack(p_im).reshape(depth, nb, groups, nstate_g),
            jnp.stack(p_cv), p_ff,
            jnp.stack(s_re).reshape(depth, ns, groups, nstate_g), jnp.stack(s_im).reshape(depth, ns, groups, nstate_g),
            s_cv, s_ff)
```

```python
import functools
import math

import jax
import jax.numpy as jnp
from jax import lax
from jax.experimental import pallas as pl
from jax.experimental.pallas import tpu as pltpu

F32 = jnp.float32
BF16 = jnp.bfloat16

LN_EPS = 1e-5
V7X_VMEM_LIMIT_BYTES = 56 * 1024 * 1024
SUBLANES = 8
LANES = 128
S5_GROUP_CH = 16
S5_STATE = 64
SUPER = 4
ROW_SPLIT = 2


def _params(sem):
    return pltpu.CompilerParams(dimension_semantics=sem, vmem_limit_bytes=V7X_VMEM_LIMIT_BYTES)


def _resident(stacked, l):
    shape = stacked.shape[1:]
    nd = len(shape)
    return pl.BlockSpec((None,) + shape, lambda *_: (l,) + (0,) * nd, pipeline_mode=pl.Buffered(1))


def _ln(r, g, b):
    mu = jnp.mean(r, axis=-1, keepdims=True)
    d = r - mu
    var = jnp.mean(d * d, axis=-1, keepdims=True)
    return d * lax.rsqrt(var + LN_EPS) * g + b


def _in_proj_kernel(x_ref, w_ref, u_ref, c_ref, *, s5w, cw):
    rs = x_ref.shape[0] // ROW_SPLIT
    for s in range(ROW_SPLIT):
        rows = slice(s * rs, (s + 1) * rs)
        xb = x_ref[rows, :].astype(BF16)
        u_ref[rows, :] = jnp.dot(xb, w_ref[:, :s5w], preferred_element_type=F32)
        cv = jnp.dot(xb, w_ref[:, s5w:s5w + cw], preferred_element_type=F32)
        cg = jnp.dot(xb, w_ref[:, s5w + cw:], preferred_element_type=F32)
        c_ref[rows, :] = cv * jax.nn.sigmoid(cg)


def _in_proj(x, w_in_bf, l, s5w, cw, tm):
    n, d = x.shape
    return pl.pallas_call(
        functools.partial(_in_proj_kernel, s5w=s5w, cw=cw),
        grid=(n // tm,),
        in_specs=[pl.BlockSpec((tm, d), lambda i: (i, 0)), _resident(w_in_bf, l)],
        out_specs=[pl.BlockSpec((tm, s5w), lambda i: (i, 0)), pl.BlockSpec((tm, cw), lambda i: (i, 0))],
        out_shape=[jax.ShapeDtypeStruct((n, s5w), F32), jax.ShapeDtypeStruct((n, cw), F32)],
        compiler_params=_params(("arbitrary",)),
        name="in_proj",
    )(x, w_in_bf)


def _s5_seq_kernel(u_ref, wb_ref, wc_ref, are_ref, aim_ref, d_ref, h0re_ref, h0im_ref,
                   y_ref, hre_ref, him_ref, sre, sim, *, tc, sgw):
    step_rows = SUBLANES
    k = pl.program_id(1)

    @pl.when(k == 0)
    def _():
        hre_ref[...] = h0re_ref[...]
        him_ref[...] = h0im_ref[...]

    u = u_ref[...]
    ub = u.astype(BF16)
    for v in range(SUPER):
        r = jnp.dot(ub[:, v * sgw:(v + 1) * sgw], wb_ref[v], preferred_element_type=F32)
        for i in range(SUBLANES):
            sre[v, pl.ds(i, tc, stride=step_rows), :] = r[:, i * LANES:(i + 1) * LANES]
            sim[v, pl.ds(i, tc, stride=step_rows), :] = r[:, (SUBLANES + i) * LANES:(SUBLANES + i + 1) * LANES]

    a_re = [are_ref[v * SUBLANES:(v + 1) * SUBLANES, :] for v in range(SUPER)]
    a_im = [aim_ref[v * SUBLANES:(v + 1) * SUBLANES, :] for v in range(SUPER)]
    init = tuple(hre_ref[0, v * SUBLANES:(v + 1) * SUBLANES, :] for v in range(SUPER)) + \
        tuple(him_ref[0, v * SUBLANES:(v + 1) * SUBLANES, :] for v in range(SUPER))

    def step(t, carry):
        row = pl.multiple_of(t * step_rows, step_rows)
        new_re, new_im = [], []
        for v in range(SUPER):
            hr, hi = carry[v], carry[SUPER + v]
            nr = a_re[v] * hr - a_im[v] * hi + sre[v, pl.ds(row, step_rows), :]
            ni = a_re[v] * hi + a_im[v] * hr + sim[v, pl.ds(row, step_rows), :]
            sre[v, pl.ds(row, step_rows), :] = nr
            sim[v, pl.ds(row, step_rows), :] = ni
            new_re.append(nr)
            new_im.append(ni)
        return tuple(new_re) + tuple(new_im)

    fin = lax.fori_loop(0, tc, step, init, unroll=8)
    for v in range(SUPER):
        hre_ref[0, v * SUBLANES:(v + 1) * SUBLANES, :] = fin[v]
        him_ref[0, v * SUBLANES:(v + 1) * SUBLANES, :] = fin[SUPER + v]

    for v in range(SUPER):
        cols = [sre[v, pl.ds(i, tc, stride=step_rows), :] for i in range(SUBLANES)]
        cols += [sim[v, pl.ds(i, tc, stride=step_rows), :] for i in range(SUBLANES)]
        hcat = jnp.concatenate(cols, axis=1).astype(BF16)
        y = jnp.dot(hcat, wc_ref[v], preferred_element_type=F32)
        y_ref[:, v * sgw:(v + 1) * sgw] = y + d_ref[:, v * sgw:(v + 1) * sgw] * u[:, v * sgw:(v + 1) * sgw]


def _s5_seq(u, wb, wc, a_re, a_im, d, h0re, h0im, l, nb, t, tc):
    s5w = u.shape[1]
    sgw = s5w // SUPER
    nk = t // tc
    rows = a_re.shape[1]
    st_spec = pl.BlockSpec((1, rows, LANES), lambda b, k: (b, 0, 0))
    return pl.pallas_call(
        functools.partial(_s5_seq_kernel, tc=tc, sgw=sgw),
        grid=(nb, nk),
        in_specs=[pl.BlockSpec((tc, s5w), lambda b, k: (b * nk + k, 0)),
                  _resident(wb, l), _resident(wc, l), _resident(a_re, l), _resident(a_im, l),
                  _resident(d, l), st_spec, st_spec],
        out_specs=[pl.BlockSpec((tc, s5w), lambda b, k: (b * nk + k, 0)), st_spec, st_spec],
        out_shape=[jax.ShapeDtypeStruct((nb * t, s5w), F32),
                   jax.ShapeDtypeStruct((nb, rows, LANES), F32), jax.ShapeDtypeStruct((nb, rows, LANES), F32)],
        scratch_shapes=[pltpu.VMEM((SUPER, tc * SUBLANES, LANES), F32),
                        pltpu.VMEM((SUPER, tc * SUBLANES, LANES), F32)],
        compiler_params=_params(("arbitrary", "arbitrary")),
        name="s5_seq",
    )(u, wb, wc, a_re, a_im, d, h0re, h0im)


def _s5_planes_kernel(u_ref, wb_ref, wc_ref, are_ref, aim_ref, d_ref, h0re_ref, h0im_ref,
                      y_ref, hre_ref, him_ref, *, nt, nseq, half):
    u = u_ref[...]
    r = jnp.dot(u.astype(BF16), wb_ref[0], preferred_element_type=F32)
    a_re, a_im = are_ref[...], aim_ref[...]
    hr, hi = h0re_ref[...], h0im_ref[...]
    hs_re, hs_im = [], []
    for t in range(nt):
        br = r[t * nseq:(t + 1) * nseq, :half]
        bi = r[t * nseq:(t + 1) * nseq, half:]
        hr, hi = a_re * hr - a_im * hi + br, a_re * hi + a_im * hr + bi
        hs_re.append(hr)
        hs_im.append(hi)
    hre_ref[...] = hr
    him_ref[...] = hi
    hcat = jnp.concatenate([jnp.concatenate(hs_re, axis=0), jnp.concatenate(hs_im, axis=0)], axis=1)
    y = jnp.dot(hcat.astype(BF16), wc_ref[0], preferred_element_type=F32)
    y_ref[...] = y + d_ref[...] * u


def _s5_planes(u, wb, wc, a_re, a_im, d, h0re, h0im, l, nt, nseq):
    n, s5w = u.shape
    sgw = s5w // SUPER
    nstate = h0re.shape[2]
    half = nstate // SUPER
    return pl.pallas_call(
        functools.partial(_s5_planes_kernel, nt=nt, nseq=nseq, half=half),
        grid=(SUPER,),
        in_specs=[pl.BlockSpec((n, sgw), lambda v: (0, v)),
                  pl.BlockSpec((None, 1) + wb.shape[2:], lambda v: (l, v, 0, 0)),
                  pl.BlockSpec((None, 1) + wc.shape[2:], lambda v: (l, v, 0, 0)),
                  pl.BlockSpec((None, 1, half), lambda v: (l, 0, v)),
                  pl.BlockSpec((None, 1, half), lambda v: (l, 0, v)),
                  pl.BlockSpec((None, 1, sgw), lambda v: (l, 0, v)),
                  pl.BlockSpec((None, nseq, half), lambda v: (l, 0, v)),
                  pl.BlockSpec((None, nseq, half), lambda v: (l, 0, v))],
        out_specs=[pl.BlockSpec((n, sgw), lambda v: (0, v)),
                   pl.BlockSpec((nseq, half), lambda v: (0, v)), pl.BlockSpec((nseq, half), lambda v: (0, v))],
        out_shape=[jax.ShapeDtypeStruct((n, s5w), F32),
                   jax.ShapeDtypeStruct((nseq, nstate), F32), jax.ShapeDtypeStruct((nseq, nstate), F32)],
        compiler_params=_params(("arbitrary",)),
        name="s5_planes",
    )(u, wb, wc, a_re, a_im, d, h0re, h0im)


CONV_ROW_TILE = 32
CONV_COL_TILE = 512


def _conv_seq_kernel(c_ref, halo_ref, w_ref, b_ref, g_ref, bt_ref, out_ref, xp_ref, sh_ref, y_ref,
                     *, tm, kk, halo, blocks_per_seq):
    i = pl.program_id(0)
    start = (i % blocks_per_seq) == 0
    ncol = c_ref.shape[1]
    xp_ref[0:halo, 0:ncol] = jnp.where(start, 0.0, halo_ref[...])
    xp_ref[halo:halo + tm, 0:ncol] = c_ref[...]
    ncopy = tm + halo - SUBLANES
    for b in range(1, SUBLANES):
        sh_ref[b - 1, :, 0:ncol] = xp_ref[pl.ds(b, ncopy), 0:ncol]
    rt = CONV_ROW_TILE
    ngrp = rt // SUBLANES
    for lc in range(ncol // CONV_COL_TILE):
        cols = slice(lc * CONV_COL_TILE, (lc + 1) * CONV_COL_TILE)

        def tile(r, carry, cols=cols):
            r0 = pl.multiple_of(r * rt, rt)
            acc = [jnp.broadcast_to(b_ref[:, cols], (SUBLANES, CONV_COL_TILE))] * ngrp
            for j in range(kk):
                a, b = divmod(halo - j, SUBLANES)
                k = kk - 1 - j
                w8 = w_ref[SUBLANES * k:SUBLANES * (k + 1), cols]
                for q in range(ngrp):
                    rows = pl.ds(r0 + SUBLANES * (a + q), SUBLANES)
                    xs = xp_ref[rows, cols] if b == 0 else sh_ref[b - 1, rows, cols]
                    acc[q] = acc[q] + w8 * xs
            for q in range(ngrp):
                y_ref[pl.ds(r0 + SUBLANES * q, SUBLANES), cols] = acc[q]
            return carry

        lax.fori_loop(0, tm // rt, tile, 0)
    out_ref[...] = jax.nn.silu(_ln(y_ref[...], g_ref[...], bt_ref[...]))


def _conv_seq(c, w, b, g, bt, l, t, tm, kk):
    n, cw = c.shape
    halo = 32
    assert kk - 1 <= halo and tm % halo == 0 and t % tm == 0
    hb = tm // halo
    return pl.pallas_call(
        functools.partial(_conv_seq_kernel, tm=tm, kk=kk, halo=halo, blocks_per_seq=t // tm),
        grid=(n // tm,),
        in_specs=[pl.BlockSpec((tm, cw), lambda i: (i, 0)),
                  pl.BlockSpec((halo, cw), lambda i: (jnp.maximum(i * hb - 1, 0), 0)),
                  _resident(w, l), _resident(b, l), _resident(g, l), _resident(bt, l)],
        out_specs=pl.BlockSpec((tm, cw), lambda i: (i, 0)),
        out_shape=jax.ShapeDtypeStruct((n, cw), F32),
        scratch_shapes=[pltpu.VMEM((tm + halo, cw + LANES), F32),
                        pltpu.VMEM((SUBLANES - 1, tm + halo - SUBLANES, cw + LANES), F32),
                        pltpu.VMEM((tm, cw), F32)],
        compiler_params=_params(("arbitrary",)),
        name="conv_seq",
    )(c, c, w, b, g, bt)


def _conv_planes_kernel(cache_ref, c_ref, w_ref, b_ref, g_ref, bt_ref, out_ref, *, nt, kk):
    rows, cw = c_ref.shape[1], c_ref.shape[2]
    gam, bet = g_ref[...], bt_ref[...]
    for t in range(nt):
        acc = jnp.broadcast_to(b_ref[...], (rows, cw))
        for k in range(kk):
            p = t + k
            xs = cache_ref[p] if p < kk - 1 else c_ref[p - (kk - 1)]
            w8 = w_ref[SUBLANES * k:SUBLANES * (k + 1), :]
            acc = acc + jnp.concatenate([w8] * (rows // SUBLANES), axis=0) * xs
        out_ref[t] = jax.nn.silu(_ln(acc, gam, bet))


def _conv_planes(cache_t, c_t, w8, b, g, bt, l, kk, rows):
    nt, nseq, cw = c_t.shape
    return pl.pallas_call(
        functools.partial(_conv_planes_kernel, nt=nt, kk=kk),
        grid=(nseq // rows,),
        in_specs=[pl.BlockSpec((None, kk - 1, rows, cw), lambda i: (l, 0, i, 0)),
                  pl.BlockSpec((nt, rows, cw), lambda i: (0, i, 0)),
                  _resident(w8, l), _resident(b, l), _resident(g, l), _resident(bt, l)],
        out_specs=pl.BlockSpec((nt, rows, cw), lambda i: (0, i, 0)),
        out_shape=jax.ShapeDtypeStruct((nt, nseq, cw), F32),
        compiler_params=_params(("arbitrary",)),
        name="conv_planes",
    )(cache_t, c_t, w8, b, g, bt)


def _mix_out_kernel(y5_ref, ca_ref, x_ref, wglu_ref, wout_ref, g_ref, b_ref, out_ref, *, alpha, s5w):
    rs = x_ref.shape[0] // ROW_SPLIT
    for s in range(ROW_SPLIT):
        rows = slice(s * rs, (s + 1) * rs)
        g5 = jax.nn.gelu(y5_ref[rows, :])
        gate = jnp.dot(g5.astype(BF16), wglu_ref[...], preferred_element_type=F32)
        s5o = g5 * jax.nn.sigmoid(gate)
        mix = jnp.dot(s5o.astype(BF16), wout_ref[:s5w, :], preferred_element_type=F32)
        mix = mix + jnp.dot(ca_ref[rows, :].astype(BF16), wout_ref[s5w:, :], preferred_element_type=F32)
        out_ref[rows, :] = _ln(alpha * x_ref[rows, :] + mix, g_ref[...], b_ref[...])


def _mix_out(y5, ca, x, wglu, wout, g, b, l, alpha, tm):
    n, d = x.shape
    s5w, cw = y5.shape[1], ca.shape[1]
    return pl.pallas_call(
        functools.partial(_mix_out_kernel, alpha=alpha, s5w=s5w),
        grid=(n // tm,),
        in_specs=[pl.BlockSpec((tm, s5w), lambda i: (i, 0)), pl.BlockSpec((tm, cw), lambda i: (i, 0)),
                  pl.BlockSpec((tm, d), lambda i: (i, 0)),
                  _resident(wglu, l), _resident(wout, l), _resident(g, l), _resident(b, l)],
        out_specs=pl.BlockSpec((tm, d), lambda i: (i, 0)),
        out_shape=jax.ShapeDtypeStruct((n, d), F32),
        compiler_params=_params(("arbitrary",)),
        name="mix_out",
    )(y5, ca, x, wglu, wout, g, b)


FFN_HALO = 16


def _ffn_kernel(*refs, alpha, tm, planes, nseq, blocks_per_seq):
    if planes:
        (x_ref, hg_c_ref, hv_c_ref, wg_ref, wv_ref, wd_ref, cwg_ref, cwv_ref, cbg_ref, cbv_ref, g_ref, b_ref,
         out_ref, tg_ref, tv_ref, xb_ref) = refs
    else:
        (x_ref, halo_ref, wg_ref, wv_ref, wd_ref, cwg_ref, cwv_ref, cbg_ref, cbv_ref, g_ref, b_ref,
         out_ref, tg_ref, tv_ref, xb_ref) = refs
    i = pl.program_id(0)
    j = pl.program_id(1)

    @pl.when(j == 0)
    def _():
        if planes:
            xb_ref[...] = x_ref[...].astype(BF16)
        else:
            start = (i % blocks_per_seq) == 0
            xb_ref[0:FFN_HALO, :] = jnp.where(start, 0.0, halo_ref[...]).astype(BF16)
            xb_ref[FFN_HALO:, :] = x_ref[...].astype(BF16)
        out_ref[...] = jnp.zeros_like(out_ref)

    xb = xb_ref[...]
    hg = jnp.dot(xb, wg_ref[...], preferred_element_type=F32)
    hv = jnp.dot(xb, wv_ref[...], preferred_element_type=F32)

    def conv3(h, hist_ref, cw_ref, cb_ref, tail_ref):
        w0, w1, w2 = cw_ref[0:1, :], cw_ref[1:2, :], cw_ref[2:3, :]
        if planes:
            hp = jnp.concatenate([hist_ref[...], h], axis=0)
            cur, prev1, prev2 = hp[2 * nseq:], hp[nseq:nseq + tm], hp[:tm]
            tail_ref[...] = h[tm - 2 * nseq:, :]
        else:
            cur = h[FFN_HALO:, :]
            prev1 = h[FFN_HALO - 1:FFN_HALO - 1 + tm, :]
            prev2 = h[FFN_HALO - 2:FFN_HALO - 2 + tm, :]
            tail_ref[...] = h[FFN_HALO + tm - SUBLANES:, :]
        return w2 * cur + w1 * prev1 + w0 * prev2 + cb_ref[...]

    cg = conv3(hg, hg_c_ref if planes else None, cwg_ref, cbg_ref, tg_ref)
    cv = conv3(hv, hv_c_ref if planes else None, cwv_ref, cbv_ref, tv_ref)
    act = (jax.nn.silu(cg) * cv).astype(BF16)
    out_ref[...] += jnp.dot(act, wd_ref[...], preferred_element_type=F32)

    @pl.when(j == pl.num_programs(1) - 1)
    def _():
        out_ref[...] = _ln(alpha * x_ref[...] + out_ref[...], g_ref[...], b_ref[...])


def _ffn(x, w_up, w_down, conv_w, conv_b, g, b, l, alpha, tm, tf, *, cache=None, t=None, nseq=None):
    n, d = x.shape
    dff = w_down.shape[1]
    nj = dff // tf
    fk = conv_w.shape[1]
    planes = cache is not None
    wspecs = [pl.BlockSpec((None, d, tf), lambda i, j: (l, 0, j)),
              pl.BlockSpec((None, d, tf), lambda i, j: (l, 0, nj + j)),
              pl.BlockSpec((None, tf, d), lambda i, j: (l, j, 0)),
              pl.BlockSpec((None, fk, tf), lambda i, j: (l, 0, j)),
              pl.BlockSpec((None, fk, tf), lambda i, j: (l, 0, nj + j)),
              pl.BlockSpec((None, 1, tf), lambda i, j: (l, 0, j)),
              pl.BlockSpec((None, 1, tf), lambda i, j: (l, 0, nj + j)),
              pl.BlockSpec((None, 1, d), lambda i, j: (l, 0, 0)), pl.BlockSpec((None, 1, d), lambda i, j: (l, 0, 0))]
    wargs = [w_up, w_up, w_down, conv_w, conv_w, conv_b, conv_b, g, b]
    if planes:
        assert n == tm
        hist = cache.shape[1]
        in_specs = [pl.BlockSpec((tm, d), lambda i, j: (i, 0)),
                    pl.BlockSpec((None, hist, tf), lambda i, j: (l, 0, j)),
                    pl.BlockSpec((None, hist, tf), lambda i, j: (l, 0, nj + j))] + wspecs
        args = [x, cache, cache] + wargs
        tail_rows, xb_rows, bps = hist, tm, 1
    else:
        hb = tm // FFN_HALO
        in_specs = [pl.BlockSpec((tm, d), lambda i, j: (i, 0), pipeline_mode=pl.Buffered(1)),
                    pl.BlockSpec((FFN_HALO, d), lambda i, j: (jnp.maximum(i * hb - 1, 0), 0))] + wspecs
        args = [x, x] + wargs
        tail_rows, xb_rows, bps = SUBLANES, tm + FFN_HALO, t // tm
    nblk = n // tm
    tail_spec = pl.BlockSpec((tail_rows, tf), lambda i, j: (i, j))
    return pl.pallas_call(
        functools.partial(_ffn_kernel, alpha=alpha, tm=tm, planes=planes, nseq=nseq, blocks_per_seq=bps),
        grid=(nblk, nj),
        in_specs=in_specs,
        out_specs=[pl.BlockSpec((tm, d), lambda i, j: (i, 0)), tail_spec, tail_spec],
        out_shape=[jax.ShapeDtypeStruct((n, d), F32),
                   jax.ShapeDtypeStruct((nblk * tail_rows, dff), F32),
                   jax.ShapeDtypeStruct((nblk * tail_rows, dff), F32)],
        scratch_shapes=[pltpu.VMEM((xb_rows, d), BF16)],
        compiler_params=_params(("arbitrary", "arbitrary")),
        name="ffn_planes" if planes else "ffn_seq",
    )(*args)


def _pe_kernel(x_ref, p_ref, wg_ref, wp_ref, g_ref, b_ref, out_ref, *, alpha):
    rs = x_ref.shape[0] // ROW_SPLIT
    for s in range(ROW_SPLIT):
        rows = slice(s * rs, (s + 1) * rs)
        x = x_ref[rows, :]
        gate = jax.nn.sigmoid(jnp.dot(x.astype(BF16), wg_ref[...], preferred_element_type=F32))
        e = jnp.dot(p_ref[rows, :].astype(BF16), wp_ref[...], preferred_element_type=F32)
        out_ref[rows, :] = _ln(alpha * x + gate * e, g_ref[...], b_ref[...])


def _pe(x, p, wg, wp, g, b, l, alpha, tm):
    n, d = x.shape
    pd = p.shape[2]
    return pl.pallas_call(
        functools.partial(_pe_kernel, alpha=alpha),
        grid=(n // tm,),
        in_specs=[pl.BlockSpec((tm, d), lambda i: (i, 0)), pl.BlockSpec((None, tm, pd), lambda i: (l, i, 0)),
                  _resident(wg, l), _resident(wp, l), _resident(g, l), _resident(b, l)],
        out_specs=pl.BlockSpec((tm, d), lambda i: (i, 0)),
        out_shape=jax.ShapeDtypeStruct((n, d), F32),
        compiler_params=_params(("arbitrary",)),
        name="pe_embed",
    )(x, p, wg, wp, g, b)


def _s5_discretise(lam_re, lam_im, log_dt, b_re, b_im):
    lr, li = lam_re.astype(F32), lam_im.astype(F32)
    dt = jnp.exp(log_dt.astype(F32))[..., None]
    mag = jnp.exp(lr * dt)
    ab_re = mag * jnp.cos(li * dt)
    ab_im = mag * jnp.sin(li * dt)
    num_re, num_im = ab_re - 1.0, ab_im
    den = lr * lr + li * li
    q_re = (num_re * lr + num_im * li) / den
    q_im = (num_im * lr - num_re * li) / den
    br, bi = b_re.astype(F32), b_im.astype(F32)
    bb_re = q_re[..., None] * br - q_im[..., None] * bi
    bb_im = q_re[..., None] * bi + q_im[..., None] * br
    return ab_re, ab_im, bb_re, bb_im


def _blockdiag_kernel(tbr_ref, tbi_ref, tcr_ref, tci_ref, eb_ref, ec_ref, wb_ref, wc_ref, *, gs, h, p):
    def place(t_ref, e_ref, rows_per_block, cols_per_block):
        tiled = jnp.dot(t_ref[...].astype(BF16), e_ref[...], preferred_element_type=F32)
        rg = lax.broadcasted_iota(jnp.int32, tiled.shape, 0) // rows_per_block
        cg = lax.broadcasted_iota(jnp.int32, tiled.shape, 1) // cols_per_block
        return jnp.where(rg == cg, tiled, 0.0).astype(BF16)

    wb_ref[:, :gs * p] = place(tbr_ref, eb_ref, h, p)
    wb_ref[:, gs * p:] = place(tbi_ref, eb_ref, h, p)
    wc_ref[:gs * p, :] = place(tcr_ref, ec_ref, p, h)
    wc_ref[gs * p:, :] = place(tci_ref, ec_ref, p, h)


def _s5_block_weights(bb_re, bb_im, c_re, c_im):
    nl, g, p, h = bb_re.shape
    gs = g // SUPER
    n = nl * SUPER
    tb = [v.transpose(0, 1, 3, 2).reshape(n, gs * h, p) for v in (bb_re, bb_im)]
    tc = [v.astype(F32).transpose(0, 1, 3, 2).reshape(n, gs * p, h) for v in (c_re, -c_im)]
    eb = jnp.tile(jnp.eye(p, dtype=BF16), (1, gs))
    ec = jnp.tile(jnp.eye(h, dtype=BF16), (1, gs))
    blk = lambda a: pl.BlockSpec((None,) + a.shape[1:], lambda i: (i, 0, 0))
    whole = lambda a: pl.BlockSpec(a.shape, lambda i: (0, 0))
    wb, wc = pl.pallas_call(
        functools.partial(_blockdiag_kernel, gs=gs, h=h, p=p),
        grid=(n,),
        in_specs=[blk(tb[0]), blk(tb[1]), blk(tc[0]), blk(tc[1]), whole(eb), whole(ec)],
        out_specs=[pl.BlockSpec((None, gs * h, 2 * gs * p), lambda i: (i, 0, 0)),
                   pl.BlockSpec((None, 2 * gs * p, gs * h), lambda i: (i, 0, 0))],
        out_shape=[jax.ShapeDtypeStruct((n, gs * h, 2 * gs * p), BF16),
                   jax.ShapeDtypeStruct((n, 2 * gs * p, gs * h), BF16)],
        compiler_params=_params(("arbitrary",)),
        name="s5_blockdiag",
    )(tb[0], tb[1], tc[0], tc[1], eb, ec)
    return wb.reshape(nl, SUPER, gs * h, 2 * gs * p), wc.reshape(nl, SUPER, 2 * gs * p, gs * h)


def _rows(v):
    return v.astype(F32).reshape(v.shape[0], 1, -1)


def kernel(x_prompt, x_sample, state_s5_re, state_s5_im, cache_conv, cache_ffn_conv, p_prompt, p_sample,
           w_in, s5_lam_re, s5_lam_im, s5_log_dt, s5_b_re, s5_b_im, s5_c_re, s5_c_im, s5_d, s5_w_glu,
           conv_w, conv_b, conv_ln_g, conv_ln_b, w_out, ln1_g, ln1_b,
           ffn_w_up, ffn_conv_w, ffn_conv_b, ffn_w_down, ln2_g, ln2_b,
           pe_w, pe_w_gate, ln3_g, ln3_b):
    nb, t, d = x_prompt.shape
    ns, ts, _ = x_sample.shape
    depth = w_in.shape[0]
    groups, nstate_g = s5_lam_re.shape[1], s5_lam_re.shape[2]
    s5w = groups * s5_d.shape[2]
    cw = d - s5w
    nstate = groups * nstate_g
    kk = conv_w.shape[1]
    fk = ffn_conv_w.shape[1]
    dff = ffn_w_down.shape[1]
    assert fk == 3 and ts >= fk - 1 and ts <= kk - 1 and t >= kk - 1
    alpha = (2.0 * depth) ** 0.25
    tm, tm_ffn, tf, tc, tconv = 512, 1024, 512, 512, 256

    xp = x_prompt.reshape(nb * t, d)
    xs = x_sample.transpose(1, 0, 2).reshape(ts * ns, d)
    zero_state = jnp.zeros((nb, nstate // LANES, LANES), F32)

    w_in_bf, wglu_bf, wout_bf = w_in.astype(BF16), s5_w_glu.astype(BF16), w_out.astype(BF16)
    wup_bf, wdown_bf = ffn_w_up.astype(BF16), ffn_w_down.astype(BF16)
    wpe_bf, wgate_bf = pe_w.astype(BF16), pe_w_gate.astype(BF16)
    ab_re, ab_im, bb_re, bb_im = _s5_discretise(s5_lam_re, s5_lam_im, s5_log_dt, s5_b_re, s5_b_im)
    wb, wc = _s5_block_weights(bb_re, bb_im, s5_c_re, s5_c_im)
    a_tile_re, a_tile_im = (v.reshape(depth, nstate // LANES, LANES) for v in (ab_re, ab_im))
    a_row_re, a_row_im = (v.reshape(depth, 1, nstate) for v in (ab_re, ab_im))
    d_rows = _rows(s5_d.reshape(depth, s5w))
    conv_w1 = conv_w.astype(F32)
    conv_w8 = jnp.repeat(conv_w1, SUBLANES, axis=1)
    conv_b_rows, cg_rows, cb_rows = _rows(conv_b), _rows(conv_ln_g), _rows(conv_ln_b)
    fcw, fcb = ffn_conv_w.astype(F32), _rows(ffn_conv_b)
    l1g, l1b, l2g, l2b, l3g, l3b = (_rows(v) for v in (ln1_g, ln1_b, ln2_g, ln2_b, ln3_g, ln3_b))
    pp = p_prompt.reshape(depth, nb * t, -1)
    ps = p_sample.transpose(0, 2, 1, 3).reshape(depth, ts * ns, -1)
    h0_re = state_s5_re.reshape(depth, ns, nstate).astype(F32)
    h0_im = state_s5_im.reshape(depth, ns, nstate).astype(F32)
    cache_t = cache_conv.transpose(0, 2, 1, 3)
    fcache = cache_ffn_conv.transpose(0, 2, 1, 3).reshape(depth, (fk - 1) * ns, 2 * dff)

    p_re, p_im, p_cv, p_tg, p_tv = [], [], [], [], []
    s_re, s_im, s_c, s_tg, s_tv = [], [], [], [], []
    for l in range(depth):
        u, c = _in_proj(xp, w_in_bf, l, s5w, cw, tm)
        y5, hre, him = _s5_seq(u, wb, wc, a_tile_re, a_tile_im, d_rows, zero_state, zero_state, l, nb, t, tc)
        ca = _conv_seq(c, conv_w8, conv_b_rows, cg_rows, cb_rows, l, t, tconv, kk)
        x1 = _mix_out(y5, ca, xp, wglu_bf, wout_bf, l1g, l1b, l, alpha, tm)
        x2, tg, tv = _ffn(x1, wup_bf, wdown_bf, fcw, fcb, l2g, l2b, l, alpha, tm_ffn, tf, t=t)
        xp = _pe(x2, pp, wgate_bf, wpe_bf, l3g, l3b, l, alpha, tm)
        p_re.append(hre)
        p_im.append(him)
        p_cv.append(jnp.stack([c[(b + 1) * t - (kk - 1):(b + 1) * t] for b in range(nb)]))
        p_tg.append(tg)
        p_tv.append(tv)

        u, c = _in_proj(xs, w_in_bf, l, s5w, cw, ts * ns)
        y5, hre, him = _s5_planes(u, wb, wc, a_row_re, a_row_im, d_rows, h0_re, h0_im, l, ts, ns)
        ca = _conv_planes(cache_t, c.reshape(ts, ns, cw), conv_w8, conv_b_rows, cg_rows, cb_rows, l, kk, 32)
        x1 = _mix_out(y5, ca.reshape(ts * ns, cw), xs, wglu_bf, wout_bf, l1g, l1b, l, alpha, ts * ns)
        x2, tg, tv = _ffn(x1, wup_bf, wdown_bf, fcw, fcb, l2g, l2b, l, alpha, ts * ns, tf, cache=fcache, nseq=ns)
        xs = _pe(x2, ps, wgate_bf, wpe_bf, l3g, l3b, l, alpha, ts * ns)
        s_re.append(hre)
        s_im.append(him)
        s_c.append(c)
        s_tg.append(tg)
        s_tv.append(tv)

    y_prompt = xp.reshape(nb, t, d)
    y_sample = xs.reshape(ts, ns, d).transpose(1, 0, 2)
    bps = t // tm_ffn
    p_tails = jnp.concatenate([jnp.stack(p_tg), jnp.stack(p_tv)], axis=2).reshape(depth, nb, bps, SUBLANES, 2 * dff)
    p_ff = p_tails[:, :, bps - 1, SUBLANES - (fk - 1):]
    s_cv = jnp.concatenate([cache_conv[:, :, ts:], jnp.stack(s_c).reshape(depth, ts, ns, cw).transpose(0, 2, 1, 3)],
                           axis=2)
    s_ff = jnp.concatenate([jnp.stack(s_tg), jnp.stack(s_tv)], axis=2)
    s_ff = s_ff.reshape(depth, fk - 1, ns, 2 * dff).transpose(0, 2, 1, 3)
    return (y_prompt, y_sample,
            jnp.stack(p_re).reshape(depth, nb, groups, nstate_g), jnp.stack(p_im).reshape(depth, nb, groups, nstate_g),
            jnp.stack(p_cv), p_ff,
            jnp.stack(s_re).reshape(depth, ns, groups, nstate_g), jnp.stack(s_im).reshape(depth, ns, groups, nstate_g),
            s_cv, s_ff)
```

```python
import functools

import jax
import jax.numpy as jnp
from jax import lax
from jax.experimental import pallas as pl
from jax.experimental.pallas import tpu as pltpu

F32 = jnp.float32
BF16 = jnp.bfloat16

LN_EPS = 1e-5
V7X_VMEM_LIMIT_BYTES = 56 * 1024 * 1024
SUBLANES = 8
LANES = 128
SUPER = 4
ROW_SPLIT = 2


def _params(sem):
    return pltpu.CompilerParams(dimension_semantics=sem, vmem_limit_bytes=V7X_VMEM_LIMIT_BYTES)


def _resident(stacked, l):
    shape = stacked.shape[1:]
    nd = len(shape)
    return pl.BlockSpec((None,) + shape, lambda *_: (l,) + (0,) * nd, pipeline_mode=pl.Buffered(1))


def _ln(r, g, b):
    mu = jnp.mean(r, axis=-1, keepdims=True)
    d = r - mu
    var = jnp.mean(d * d, axis=-1, keepdims=True)
    return d * lax.rsqrt(var + LN_EPS) * g + b


def _pick_rows(first_source, a_ref, b_ref, rows):
    return jnp.where(first_source, a_ref[rows, :], b_ref[rows, :])


def _in_proj_kernel(xa_ref, xb_ref, w_ref, u_ref, c_ref, *, s5w, cw, n_a):
    first = pl.program_id(0) < n_a
    rs = xa_ref.shape[0] // ROW_SPLIT
    for s in range(ROW_SPLIT):
        rows = slice(s * rs, (s + 1) * rs)
        xb = _pick_rows(first, xa_ref, xb_ref, rows).astype(BF16)
        u_ref[rows, :] = jnp.dot(xb, w_ref[:, :s5w], preferred_element_type=F32)
        cv = jnp.dot(xb, w_ref[:, s5w:s5w + cw], preferred_element_type=F32)
        cg = jnp.dot(xb, w_ref[:, s5w + cw:], preferred_element_type=F32)
        c_ref[rows, :] = cv * jax.nn.sigmoid(cg)


def _two_sources(n_a, b_blk, tm, width):
    return [pl.BlockSpec((tm, width), lambda i: (jnp.minimum(i, n_a - 1), 0)),
            pl.BlockSpec((tm, width), lambda i: (b_blk, 0))]


def _in_proj(xa, n_a, xb, b_blk, w_in_bf, l, s5w, cw, tm):
    d = xa.shape[1]
    n = (n_a + 1) * tm
    return pl.pallas_call(
        functools.partial(_in_proj_kernel, s5w=s5w, cw=cw, n_a=n_a),
        grid=(n_a + 1,),
        in_specs=_two_sources(n_a, b_blk, tm, d) + [_resident(w_in_bf, l)],
        out_specs=[pl.BlockSpec((tm, s5w), lambda i: (i, 0)), pl.BlockSpec((tm, cw), lambda i: (i, 0))],
        out_shape=[jax.ShapeDtypeStruct((n, s5w), F32), jax.ShapeDtypeStruct((n, cw), F32)],
        compiler_params=_params(("arbitrary",)),
        name="in_proj",
    )(xa, xb, w_in_bf)


def _s5_seq_kernel(u_ref, wb_ref, wc_ref, are_ref, aim_ref, d_ref, h0re_ref, h0im_ref,
                   y_ref, hre_ref, him_ref, sre, sim, *, tc, sgw):
    step_rows = SUBLANES
    k = pl.program_id(1)

    @pl.when(k == 0)
    def _():
        hre_ref[...] = h0re_ref[...]
        him_ref[...] = h0im_ref[...]

    u = u_ref[...]
    ub = u.astype(BF16)
    for v in range(SUPER):
        r = jnp.dot(ub[:, v * sgw:(v + 1) * sgw], wb_ref[v], preferred_element_type=F32)
        for i in range(SUBLANES):
            sre[v, pl.ds(i, tc, stride=step_rows), :] = r[:, i * LANES:(i + 1) * LANES]
            sim[v, pl.ds(i, tc, stride=step_rows), :] = r[:, (SUBLANES + i) * LANES:(SUBLANES + i + 1) * LANES]

    a_re = [are_ref[v * SUBLANES:(v + 1) * SUBLANES, :] for v in range(SUPER)]
    a_im = [aim_ref[v * SUBLANES:(v + 1) * SUBLANES, :] for v in range(SUPER)]
    init = tuple(hre_ref[0, v * SUBLANES:(v + 1) * SUBLANES, :] for v in range(SUPER)) + \
        tuple(him_ref[0, v * SUBLANES:(v + 1) * SUBLANES, :] for v in range(SUPER))

    def step(t, carry):
        row = pl.multiple_of(t * step_rows, step_rows)
        new_re, new_im = [], []
        for v in range(SUPER):
            hr, hi = carry[v], carry[SUPER + v]
            nr = a_re[v] * hr - a_im[v] * hi + sre[v, pl.ds(row, step_rows), :]
            ni = a_re[v] * hi + a_im[v] * hr + sim[v, pl.ds(row, step_rows), :]
            sre[v, pl.ds(row, step_rows), :] = nr
            sim[v, pl.ds(row, step_rows), :] = ni
            new_re.append(nr)
            new_im.append(ni)
        return tuple(new_re) + tuple(new_im)

    fin = lax.fori_loop(0, tc, step, init, unroll=8)
    for v in range(SUPER):
        hre_ref[0, v * SUBLANES:(v + 1) * SUBLANES, :] = fin[v]
        him_ref[0, v * SUBLANES:(v + 1) * SUBLANES, :] = fin[SUPER + v]

    for v in range(SUPER):
        cols = [sre[v, pl.ds(i, tc, stride=step_rows), :] for i in range(SUBLANES)]
        cols += [sim[v, pl.ds(i, tc, stride=step_rows), :] for i in range(SUBLANES)]
        hcat = jnp.concatenate(cols, axis=1).astype(BF16)
        y = jnp.dot(hcat, wc_ref[v], preferred_element_type=F32)
        y_ref[:, v * sgw:(v + 1) * sgw] = y + d_ref[:, v * sgw:(v + 1) * sgw] * u[:, v * sgw:(v + 1) * sgw]


def _s5_seq(u, wb, wc, a_re, a_im, d, h0re, h0im, l, nb, t, tc):
    s5w = u.shape[1]
    sgw = s5w // SUPER
    nk = t // tc
    rows = a_re.shape[1]
    st_spec = pl.BlockSpec((1, rows, LANES), lambda b, k: (b, 0, 0))
    return pl.pallas_call(
        functools.partial(_s5_seq_kernel, tc=tc, sgw=sgw),
        grid=(nb, nk),
        in_specs=[pl.BlockSpec((tc, s5w), lambda b, k: (b * nk + k, 0)),
                  _resident(wb, l), _resident(wc, l), _resident(a_re, l), _resident(a_im, l),
                  _resident(d, l), st_spec, st_spec],
        out_specs=[pl.BlockSpec((tc, s5w), lambda b, k: (b * nk + k, 0)), st_spec, st_spec],
        out_shape=[jax.ShapeDtypeStruct((nb * t, s5w), F32),
                   jax.ShapeDtypeStruct((nb, rows, LANES), F32), jax.ShapeDtypeStruct((nb, rows, LANES), F32)],
        scratch_shapes=[pltpu.VMEM((SUPER, tc * SUBLANES, LANES), F32),
                        pltpu.VMEM((SUPER, tc * SUBLANES, LANES), F32)],
        compiler_params=_params(("arbitrary", "arbitrary")),
        name="s5_seq",
    )(u, wb, wc, a_re, a_im, d, h0re, h0im)


def _s5_planes_kernel(u_ref, wb_ref, wc_ref, are_ref, aim_ref, d_ref, h0re_ref, h0im_ref,
                      y_ref, hre_ref, him_ref, *, nt, nseq, half):
    u = u_ref[...]
    r = jnp.dot(u.astype(BF16), wb_ref[0], preferred_element_type=F32)
    a_re, a_im = are_ref[...], aim_ref[...]
    hr, hi = h0re_ref[...], h0im_ref[...]
    hs_re, hs_im = [], []
    for t in range(nt):
        br = r[t * nseq:(t + 1) * nseq, :half]
        bi = r[t * nseq:(t + 1) * nseq, half:]
        hr, hi = a_re * hr - a_im * hi + br, a_re * hi + a_im * hr + bi
        hs_re.append(hr)
        hs_im.append(hi)
    hre_ref[...] = hr
    him_ref[...] = hi
    hcat = jnp.concatenate([jnp.concatenate(hs_re, axis=0), jnp.concatenate(hs_im, axis=0)], axis=1)
    y = jnp.dot(hcat.astype(BF16), wc_ref[0], preferred_element_type=F32)
    y_ref[...] = y + d_ref[...] * u


def _s5_planes(u, u_blk, wb, wc, a_re, a_im, d, h0re, h0im, l, nt, nseq):
    n, s5w = nt * nseq, u.shape[1]
    sgw = s5w // SUPER
    nstate = h0re.shape[2]
    half = nstate // SUPER
    return pl.pallas_call(
        functools.partial(_s5_planes_kernel, nt=nt, nseq=nseq, half=half),
        grid=(SUPER,),
        in_specs=[pl.BlockSpec((n, sgw), lambda v: (u_blk, v)),
                  pl.BlockSpec((None, 1) + wb.shape[2:], lambda v: (l, v, 0, 0)),
                  pl.BlockSpec((None, 1) + wc.shape[2:], lambda v: (l, v, 0, 0)),
                  pl.BlockSpec((None, 1, half), lambda v: (l, 0, v)),
                  pl.BlockSpec((None, 1, half), lambda v: (l, 0, v)),
                  pl.BlockSpec((None, 1, sgw), lambda v: (l, 0, v)),
                  pl.BlockSpec((None, nseq, half), lambda v: (l, 0, v)),
                  pl.BlockSpec((None, nseq, half), lambda v: (l, 0, v))],
        out_specs=[pl.BlockSpec((n, sgw), lambda v: (0, v)),
                   pl.BlockSpec((nseq, half), lambda v: (0, v)), pl.BlockSpec((nseq, half), lambda v: (0, v))],
        out_shape=[jax.ShapeDtypeStruct((n, s5w), F32),
                   jax.ShapeDtypeStruct((nseq, nstate), F32), jax.ShapeDtypeStruct((nseq, nstate), F32)],
        compiler_params=_params(("arbitrary",)),
        name="s5_planes",
    )(u, wb, wc, a_re, a_im, d, h0re, h0im)


CONV_ROW_TILE = 32
CONV_COL_TILE = 512


def _conv_seq_kernel(c_ref, halo_ref, w_ref, b_ref, g_ref, bt_ref, out_ref, xp_ref, sh_ref, y_ref,
                     *, tm, kk, halo, blocks_per_seq):
    i = pl.program_id(0)
    start = (i % blocks_per_seq) == 0
    ncol = c_ref.shape[1]
    xp_ref[0:halo, 0:ncol] = jnp.where(start, 0.0, halo_ref[...])
    xp_ref[halo:halo + tm, 0:ncol] = c_ref[...]
    ncopy = tm + halo - SUBLANES
    for b in range(1, SUBLANES):
        sh_ref[b - 1, :, 0:ncol] = xp_ref[pl.ds(b, ncopy), 0:ncol]
    rt = CONV_ROW_TILE
    ngrp = rt // SUBLANES
    for lc in range(ncol // CONV_COL_TILE):
        cols = slice(lc * CONV_COL_TILE, (lc + 1) * CONV_COL_TILE)

        def tile(r, carry, cols=cols):
            r0 = pl.multiple_of(r * rt, rt)
            acc = [jnp.broadcast_to(b_ref[:, cols], (SUBLANES, CONV_COL_TILE))] * ngrp
            for j in range(kk):
                a, b = divmod(halo - j, SUBLANES)
                k = kk - 1 - j
                w8 = w_ref[SUBLANES * k:SUBLANES * (k + 1), cols]
                for q in range(ngrp):
                    rows = pl.ds(r0 + SUBLANES * (a + q), SUBLANES)
                    xs = xp_ref[rows, cols] if b == 0 else sh_ref[b - 1, rows, cols]
                    acc[q] = acc[q] + w8 * xs
            for q in range(ngrp):
                y_ref[pl.ds(r0 + SUBLANES * q, SUBLANES), cols] = acc[q]
            return carry

        lax.fori_loop(0, tm // rt, tile, 0)
    out_ref[...] = jax.nn.silu(_ln(y_ref[...], g_ref[...], bt_ref[...]))


def _conv_seq(c, n, w, b, g, bt, l, t, tm, kk):
    cw = c.shape[1]
    halo = 32
    assert kk - 1 <= halo and tm % halo == 0 and t % tm == 0
    hb = tm // halo
    return pl.pallas_call(
        functools.partial(_conv_seq_kernel, tm=tm, kk=kk, halo=halo, blocks_per_seq=t // tm),
        grid=(n // tm,),
        in_specs=[pl.BlockSpec((tm, cw), lambda i: (i, 0)),
                  pl.BlockSpec((halo, cw), lambda i: (jnp.maximum(i * hb - 1, 0), 0)),
                  _resident(w, l), _resident(b, l), _resident(g, l), _resident(bt, l)],
        out_specs=pl.BlockSpec((tm, cw), lambda i: (i, 0)),
        out_shape=jax.ShapeDtypeStruct((n, cw), F32),
        scratch_shapes=[pltpu.VMEM((tm + halo, cw + LANES), F32),
                        pltpu.VMEM((SUBLANES - 1, tm + halo - SUBLANES, cw + LANES), F32),
                        pltpu.VMEM((tm, cw), F32)],
        compiler_params=_params(("arbitrary",)),
        name="conv_seq",
    )(c, c, w, b, g, bt)


def _conv_planes_kernel(cache_ref, c_ref, w_ref, b_ref, g_ref, bt_ref, out_ref, *, nt, kk):
    rows, cw = c_ref.shape[1], c_ref.shape[2]
    gam, bet = g_ref[...], bt_ref[...]
    for t in range(nt):
        acc = jnp.broadcast_to(b_ref[...], (rows, cw))
        for k in range(kk):
            p = t + k
            xs = cache_ref[p] if p < kk - 1 else c_ref[p - (kk - 1)]
            w8 = w_ref[SUBLANES * k:SUBLANES * (k + 1), :]
            acc = acc + jnp.concatenate([w8] * (rows // SUBLANES), axis=0) * xs
        out_ref[t] = jax.nn.silu(_ln(acc, gam, bet))


def _conv_planes(cache_t, c_t, w8, b, g, bt, l, kk, rows):
    nt, nseq, cw = c_t.shape
    return pl.pallas_call(
        functools.partial(_conv_planes_kernel, nt=nt, kk=kk),
        grid=(nseq // rows,),
        in_specs=[pl.BlockSpec((None, kk - 1, rows, cw), lambda i: (l, 0, i, 0)),
                  pl.BlockSpec((nt, rows, cw), lambda i: (0, i, 0)),
                  _resident(w8, l), _resident(b, l), _resident(g, l), _resident(bt, l)],
        out_specs=pl.BlockSpec((nt, rows, cw), lambda i: (0, i, 0)),
        out_shape=jax.ShapeDtypeStruct((nt, nseq, cw), F32),
        compiler_params=_params(("arbitrary",)),
        name="conv_planes",
    )(cache_t, c_t, w8, b, g, bt)


def _mix_out_kernel(y5a_ref, y5b_ref, caa_ref, cab_ref, xa_ref, xb_ref, wglu_ref, wout_ref, g_ref, b_ref, out_ref,
                    *, alpha, s5w, n_a):
    first = pl.program_id(0) < n_a
    rs = xa_ref.shape[0] // ROW_SPLIT
    for s in range(ROW_SPLIT):
        rows = slice(s * rs, (s + 1) * rs)
        g5 = jax.nn.gelu(_pick_rows(first, y5a_ref, y5b_ref, rows))
        gate = jnp.dot(g5.astype(BF16), wglu_ref[...], preferred_element_type=F32)
        s5o = g5 * jax.nn.sigmoid(gate)
        mix = jnp.dot(s5o.astype(BF16), wout_ref[:s5w, :], preferred_element_type=F32)
        ca = _pick_rows(first, caa_ref, cab_ref, rows)
        mix = mix + jnp.dot(ca.astype(BF16), wout_ref[s5w:, :], preferred_element_type=F32)
        x = _pick_rows(first, xa_ref, xb_ref, rows)
        out_ref[rows, :] = _ln(alpha * x + mix, g_ref[...], b_ref[...])


def _mix_out(y5a, y5b, caa, cab, xa, n_a, xb, b_blk, wglu, wout, g, b, l, alpha, tm):
    d = xa.shape[1]
    s5w, cw = y5a.shape[1], caa.shape[1]
    n = (n_a + 1) * tm
    return pl.pallas_call(
        functools.partial(_mix_out_kernel, alpha=alpha, s5w=s5w, n_a=n_a),
        grid=(n_a + 1,),
        in_specs=_two_sources(n_a, 0, tm, s5w) + _two_sources(n_a, 0, tm, cw) + _two_sources(n_a, b_blk, tm, d) +
        [_resident(wglu, l), _resident(wout, l), _resident(g, l), _resident(b, l)],
        out_specs=pl.BlockSpec((tm, d), lambda i: (i, 0)),
        out_shape=jax.ShapeDtypeStruct((n, d), F32),
        compiler_params=_params(("arbitrary",)),
        name="mix_out",
    )(y5a, y5b, caa, cab, xa, xb, wglu, wout, g, b)


FFN_HALO = 16


def _ffn_kernel(*refs, alpha, tm, planes, nseq, blocks_per_seq):
    if planes:
        (x_ref, hg_c_ref, hv_c_ref, wg_ref, wv_ref, wd_ref, cwg_ref, cwv_ref, cbg_ref, cbv_ref, g_ref, b_ref,
         out_ref, tg_ref, tv_ref, xb_ref) = refs
    else:
        (x_ref, halo_ref, wg_ref, wv_ref, wd_ref, cwg_ref, cwv_ref, cbg_ref, cbv_ref, g_ref, b_ref,
         out_ref, tg_ref, tv_ref, xb_ref) = refs
    i = pl.program_id(0)
    j = pl.program_id(1)

    @pl.when(j == 0)
    def _():
        if planes:
            xb_ref[...] = x_ref[...].astype(BF16)
        else:
            start = (i % blocks_per_seq) == 0
            xb_ref[0:FFN_HALO, :] = jnp.where(start, 0.0, halo_ref[...]).astype(BF16)
            xb_ref[FFN_HALO:, :] = x_ref[...].astype(BF16)
        out_ref[...] = jnp.zeros_like(out_ref)

    xb = xb_ref[...]
    hg = jnp.dot(xb, wg_ref[...], preferred_element_type=F32)
    hv = jnp.dot(xb, wv_ref[...], preferred_element_type=F32)

    def conv3(h, hist_ref, cw_ref, cb_ref, tail_ref):
        w0, w1, w2 = cw_ref[0:1, :], cw_ref[1:2, :], cw_ref[2:3, :]
        if planes:
            hp = jnp.concatenate([hist_ref[...], h], axis=0)
            cur, prev1, prev2 = hp[2 * nseq:], hp[nseq:nseq + tm], hp[:tm]
            tail_ref[...] = h[tm - 2 * nseq:, :]
        else:
            cur = h[FFN_HALO:, :]
            prev1 = h[FFN_HALO - 1:FFN_HALO - 1 + tm, :]
            prev2 = h[FFN_HALO - 2:FFN_HALO - 2 + tm, :]
            tail_ref[...] = h[FFN_HALO + tm - SUBLANES:, :]
        return w2 * cur + w1 * prev1 + w0 * prev2 + cb_ref[...]

    cg = conv3(hg, hg_c_ref if planes else None, cwg_ref, cbg_ref, tg_ref)
    cv = conv3(hv, hv_c_ref if planes else None, cwv_ref, cbv_ref, tv_ref)
    act = (jax.nn.silu(cg) * cv).astype(BF16)
    out_ref[...] += jnp.dot(act, wd_ref[...], preferred_element_type=F32)

    @pl.when(j == pl.num_programs(1) - 1)
    def _():
        out_ref[...] = _ln(alpha * x_ref[...] + out_ref[...], g_ref[...], b_ref[...])


def _ffn(x, n, w_up, w_down, conv_w, conv_b, g, b, l, alpha, tm, tf, *, cache=None, x_blk=0, t=None, nseq=None):
    d = x.shape[1]
    dff = w_down.shape[1]
    nj = dff // tf
    fk = conv_w.shape[1]
    planes = cache is not None
    wspecs = [pl.BlockSpec((None, d, tf), lambda i, j: (l, 0, j)),
              pl.BlockSpec((None, d, tf), lambda i, j: (l, 0, nj + j)),
              pl.BlockSpec((None, tf, d), lambda i, j: (l, j, 0)),
              pl.BlockSpec((None, fk, tf), lambda i, j: (l, 0, j)),
              pl.BlockSpec((None, fk, tf), lambda i, j: (l, 0, nj + j)),
              pl.BlockSpec((None, 1, tf), lambda i, j: (l, 0, j)),
              pl.BlockSpec((None, 1, tf), lambda i, j: (l, 0, nj + j)),
              pl.BlockSpec((None, 1, d), lambda i, j: (l, 0, 0)), pl.BlockSpec((None, 1, d), lambda i, j: (l, 0, 0))]
    wargs = [w_up, w_up, w_down, conv_w, conv_w, conv_b, conv_b, g, b]
    if planes:
        assert n == tm
        hist = cache.shape[1]
        in_specs = [pl.BlockSpec((tm, d), lambda i, j: (x_blk, 0)),
                    pl.BlockSpec((None, hist, tf), lambda i, j: (l, 0, j)),
                    pl.BlockSpec((None, hist, tf), lambda i, j: (l, 0, nj + j))] + wspecs
        args = [x, cache, cache] + wargs
        tail_rows, xb_rows, bps = hist, tm, 1
    else:
        hb = tm // FFN_HALO
        in_specs = [pl.BlockSpec((tm, d), lambda i, j: (i, 0), pipeline_mode=pl.Buffered(1)),
                    pl.BlockSpec((FFN_HALO, d), lambda i, j: (jnp.maximum(i * hb - 1, 0), 0))] + wspecs
        args = [x, x] + wargs
        tail_rows, xb_rows, bps = SUBLANES, tm + FFN_HALO, t // tm
    nblk = n // tm
    tail_spec = pl.BlockSpec((tail_rows, tf), lambda i, j: (i, j))
    return pl.pallas_call(
        functools.partial(_ffn_kernel, alpha=alpha, tm=tm, planes=planes, nseq=nseq, blocks_per_seq=bps),
        grid=(nblk, nj),
        in_specs=in_specs,
        out_specs=[pl.BlockSpec((tm, d), lambda i, j: (i, 0)), tail_spec, tail_spec],
        out_shape=[jax.ShapeDtypeStruct((n, d), F32),
                   jax.ShapeDtypeStruct((nblk * tail_rows, dff), F32),
                   jax.ShapeDtypeStruct((nblk * tail_rows, dff), F32)],
        scratch_shapes=[pltpu.VMEM((xb_rows, d), BF16)],
        compiler_params=_params(("arbitrary", "arbitrary")),
        name="ffn_planes" if planes else "ffn_seq",
    )(*args)


def _pe_kernel(x_ref, p_ref, wg_ref, wp_ref, g_ref, b_ref, out_ref, *, alpha):
    rs = x_ref.shape[0] // ROW_SPLIT
    for s in range(ROW_SPLIT):
        rows = slice(s * rs, (s + 1) * rs)
        x = x_ref[rows, :]
        gate = jax.nn.sigmoid(jnp.dot(x.astype(BF16), wg_ref[...], preferred_element_type=F32))
        e = jnp.dot(p_ref[rows, :].astype(BF16), wp_ref[...], preferred_element_type=F32)
        out_ref[rows, :] = _ln(alpha * x + gate * e, g_ref[...], b_ref[...])


def _pe(x, p, wg, wp, g, b, l, alpha, tm):
    n, d = x.shape
    pd = p.shape[2]
    return pl.pallas_call(
        functools.partial(_pe_kernel, alpha=alpha),
        grid=(n // tm,),
        in_specs=[pl.BlockSpec((tm, d), lambda i: (i, 0)), pl.BlockSpec((None, tm, pd), lambda i: (l, i, 0)),
                  _resident(wg, l), _resident(wp, l), _resident(g, l), _resident(b, l)],
        out_specs=pl.BlockSpec((tm, d), lambda i: (i, 0)),
        out_shape=jax.ShapeDtypeStruct((n, d), F32),
        compiler_params=_params(("arbitrary",)),
        name="pe_embed",
    )(x, p, wg, wp, g, b)


def _pe2_kernel(xa_ref, xb_ref, pa_ref, pb_ref, wg_ref, wp_ref, g_ref, b_ref, out_ref, *, alpha, n_a):
    first = pl.program_id(0) < n_a
    rs = xa_ref.shape[0] // ROW_SPLIT
    for s in range(ROW_SPLIT):
        rows = slice(s * rs, (s + 1) * rs)
        x = _pick_rows(first, xa_ref, xb_ref, rows)
        gate = jax.nn.sigmoid(jnp.dot(x.astype(BF16), wg_ref[...], preferred_element_type=F32))
        e = jnp.dot(_pick_rows(first, pa_ref, pb_ref, rows).astype(BF16), wp_ref[...], preferred_element_type=F32)
        out_ref[rows, :] = _ln(alpha * x + gate * e, g_ref[...], b_ref[...])


def _pe2(xa, xb, pa, pb, wg, wp, g, b, l, alpha, tm):
    d = xa.shape[1]
    pd = pa.shape[2]
    n_a = xa.shape[0] // tm
    n = (n_a + 1) * tm
    return pl.pallas_call(
        functools.partial(_pe2_kernel, alpha=alpha, n_a=n_a),
        grid=(n_a + 1,),
        in_specs=_two_sources(n_a, 0, tm, d) +
        [pl.BlockSpec((None, tm, pd), lambda i: (l, jnp.minimum(i, n_a - 1), 0)),
         pl.BlockSpec((None, tm, pd), lambda i: (l, 0, 0)),
         _resident(wg, l), _resident(wp, l), _resident(g, l), _resident(b, l)],
        out_specs=pl.BlockSpec((tm, d), lambda i: (i, 0)),
        out_shape=jax.ShapeDtypeStruct((n, d), F32),
        compiler_params=_params(("arbitrary",)),
        name="pe_embed2",
    )(xa, xb, pa, pb, wg, wp, g, b)


def _s5_discretise(lam_re, lam_im, log_dt, b_re, b_im):
    lr, li = lam_re.astype(F32), lam_im.astype(F32)
    dt = jnp.exp(log_dt.astype(F32))[..., None]
    mag = jnp.exp(lr * dt)
    ab_re = mag * jnp.cos(li * dt)
    ab_im = mag * jnp.sin(li * dt)
    num_re, num_im = ab_re - 1.0, ab_im
    den = lr * lr + li * li
    q_re = (num_re * lr + num_im * li) / den
    q_im = (num_im * lr - num_re * li) / den
    br, bi = b_re.astype(F32), b_im.astype(F32)
    bb_re = q_re[..., None] * br - q_im[..., None] * bi
    bb_im = q_re[..., None] * bi + q_im[..., None] * br
    return ab_re, ab_im, bb_re, bb_im


def _blockdiag_kernel(tbr_ref, tbi_ref, tcr_ref, tci_ref, eb_ref, ec_ref, wb_ref, wc_ref, *, gs, h, p):
    def place(t_ref, e_ref, rows_per_block, cols_per_block):
        tiled = jnp.dot(t_ref[...].astype(BF16), e_ref[...], preferred_element_type=F32)
        rg = lax.broadcasted_iota(jnp.int32, tiled.shape, 0) // rows_per_block
        cg = lax.broadcasted_iota(jnp.int32, tiled.shape, 1) // cols_per_block
        return jnp.where(rg == cg, tiled, 0.0).astype(BF16)

    wb_ref[:, :gs * p] = place(tbr_ref, eb_ref, h, p)
    wb_ref[:, gs * p:] = place(tbi_ref, eb_ref, h, p)
    wc_ref[:gs * p, :] = place(tcr_ref, ec_ref, p, h)
    wc_ref[gs * p:, :] = place(tci_ref, ec_ref, p, h)


def _s5_block_weights(bb_re, bb_im, c_re, c_im):
    nl, g, p, h = bb_re.shape
    gs = g // SUPER
    n = nl * SUPER
    tb = [v.transpose(0, 1, 3, 2).reshape(n, gs * h, p) for v in (bb_re, bb_im)]
    tc = [v.astype(F32).transpose(0, 1, 3, 2).reshape(n, gs * p, h) for v in (c_re, -c_im)]
    eb = jnp.tile(jnp.eye(p, dtype=BF16), (1, gs))
    ec = jnp.tile(jnp.eye(h, dtype=BF16), (1, gs))
    blk = lambda a: pl.BlockSpec((None,) + a.shape[1:], lambda i: (i, 0, 0))
    whole = lambda a: pl.BlockSpec(a.shape, lambda i: (0, 0))
    wb, wc = pl.pallas_call(
        functools.partial(_blockdiag_kernel, gs=gs, h=h, p=p),
        grid=(n,),
        in_specs=[blk(tb[0]), blk(tb[1]), blk(tc[0]), blk(tc[1]), whole(eb), whole(ec)],
        out_specs=[pl.BlockSpec((None, gs * h, 2 * gs * p), lambda i: (i, 0, 0)),
                   pl.BlockSpec((None, 2 * gs * p, gs * h), lambda i: (i, 0, 0))],
        out_shape=[jax.ShapeDtypeStruct((n, gs * h, 2 * gs * p), BF16),
                   jax.ShapeDtypeStruct((n, 2 * gs * p, gs * h), BF16)],
        compiler_params=_params(("arbitrary",)),
        name="s5_blockdiag",
    )(tb[0], tb[1], tc[0], tc[1], eb, ec)
    return wb.reshape(nl, SUPER, gs * h, 2 * gs * p), wc.reshape(nl, SUPER, 2 * gs * p, gs * h)


def _rows(v):
    return v.astype(F32).reshape(v.shape[0], 1, -1)


def kernel(x_prompt, x_sample, state_s5_re, state_s5_im, cache_conv, cache_ffn_conv, p_prompt, p_sample,
           w_in, s5_lam_re, s5_lam_im, s5_log_dt, s5_b_re, s5_b_im, s5_c_re, s5_c_im, s5_d, s5_w_glu,
           conv_w, conv_b, conv_ln_g, conv_ln_b, w_out, ln1_g, ln1_b,
           ffn_w_up, ffn_conv_w, ffn_conv_b, ffn_w_down, ln2_g, ln2_b,
           pe_w, pe_w_gate, ln3_g, ln3_b):
    nb, t, d = x_prompt.shape
    ns, ts, _ = x_sample.shape
    depth = w_in.shape[0]
    groups, nstate_g = s5_lam_re.shape[1], s5_lam_re.shape[2]
    s5w = groups * s5_d.shape[2]
    cw = d - s5w
    nstate = groups * nstate_g
    kk = conv_w.shape[1]
    fk = ffn_conv_w.shape[1]
    dff = ffn_w_down.shape[1]
    assert fk == 3 and ts >= fk - 1 and ts <= kk - 1 and t >= kk - 1
    alpha = (2.0 * depth) ** 0.25
    tm, tm_ffn, tf, tc, tconv = 512, 1024, 512, 512, 256

    xp = x_prompt.reshape(nb * t, d)
    xs = x_sample.transpose(1, 0, 2).reshape(ts * ns, d)
    zero_state = jnp.zeros((nb, nstate // LANES, LANES), F32)

    w_in_bf, wglu_bf, wout_bf = w_in.astype(BF16), s5_w_glu.astype(BF16), w_out.astype(BF16)
    wup_bf, wdown_bf = ffn_w_up.astype(BF16), ffn_w_down.astype(BF16)
    wpe_bf, wgate_bf = pe_w.astype(BF16), pe_w_gate.astype(BF16)
    ab_re, ab_im, bb_re, bb_im = _s5_discretise(s5_lam_re, s5_lam_im, s5_log_dt, s5_b_re, s5_b_im)
    wb, wc = _s5_block_weights(bb_re, bb_im, s5_c_re, s5_c_im)
    a_tile_re, a_tile_im = (v.reshape(depth, nstate // LANES, LANES) for v in (ab_re, ab_im))
    a_row_re, a_row_im = (v.reshape(depth, 1, nstate) for v in (ab_re, ab_im))
    d_rows = _rows(s5_d.reshape(depth, s5w))
    conv_w1 = conv_w.astype(F32)
    conv_w8 = jnp.repeat(conv_w1, SUBLANES, axis=1)
    conv_b_rows, cg_rows, cb_rows = _rows(conv_b), _rows(conv_ln_g), _rows(conv_ln_b)
    fcw, fcb = ffn_conv_w.astype(F32), _rows(ffn_conv_b)
    l1g, l1b, l2g, l2b, l3g, l3b = (_rows(v) for v in (ln1_g, ln1_b, ln2_g, ln2_b, ln3_g, ln3_b))
    pp = p_prompt.reshape(depth, nb * t, -1)
    ps = p_sample.transpose(0, 2, 1, 3).reshape(depth, ts * ns, -1)
    h0_re = state_s5_re.reshape(depth, ns, nstate).astype(F32)
    h0_im = state_s5_im.reshape(depth, ns, nstate).astype(F32)
    cache_t = cache_conv.transpose(0, 2, 1, 3)
    fcache = cache_ffn_conv.transpose(0, 2, 1, 3).reshape(depth, (fk - 1) * ns, 2 * dff)

    p_re, p_im, p_cv, p_tg, p_tv = [], [], [], [], []
    s_re, s_im, s_c, s_tg, s_tv = [], [], [], [], []
    n_p, n_s = nb * t, ts * ns
    assert n_s == tm and n_p % tm == 0
    nblk_p = n_p // tm
    x_seq, x_pl, x_pl_blk = xp, xs, 0
    for l in range(depth):
        u, c = _in_proj(x_seq, nblk_p, x_pl, x_pl_blk, w_in_bf, l, s5w, cw, tm)
        c_pl = c[n_p:]
        y5p, hre_p, him_p = _s5_seq(u, wb, wc, a_tile_re, a_tile_im, d_rows, zero_state, zero_state, l, nb, t, tc)
        y5s, hre_s, him_s = _s5_planes(u, nblk_p, wb, wc, a_row_re, a_row_im, d_rows, h0_re, h0_im, l, ts, ns)
        cap = _conv_seq(c, n_p, conv_w8, conv_b_rows, cg_rows, cb_rows, l, t, tconv, kk)
        cas = _conv_planes(cache_t, c_pl.reshape(ts, ns, cw), conv_w8, conv_b_rows, cg_rows, cb_rows, l, kk, 32)
        x1 = _mix_out(y5p, y5s, cap, cas.reshape(n_s, cw), x_seq, nblk_p, x_pl, x_pl_blk,
                      wglu_bf, wout_bf, l1g, l1b, l, alpha, tm)
        x2p, tg_p, tv_p = _ffn(x1, n_p, wup_bf, wdown_bf, fcw, fcb, l2g, l2b, l, alpha, tm_ffn, tf, t=t)
        x2s, tg_s, tv_s = _ffn(x1, n_s, wup_bf, wdown_bf, fcw, fcb, l2g, l2b, l, alpha, n_s, tf,
                               cache=fcache, x_blk=nblk_p, nseq=ns)
        if l + 1 < depth:
            x_all = _pe2(x2p, x2s, pp, ps, wgate_bf, wpe_bf, l3g, l3b, l, alpha, tm)
            x_seq, x_pl, x_pl_blk = x_all, x_all, nblk_p
        else:
            xp = _pe(x2p, pp, wgate_bf, wpe_bf, l3g, l3b, l, alpha, tm)
            xs = _pe(x2s, ps, wgate_bf, wpe_bf, l3g, l3b, l, alpha, tm)
        p_re.append(hre_p)
        p_im.append(him_p)
        p_cv.append(jnp.stack([c[(b + 1) * t - (kk - 1):(b + 1) * t] for b in range(nb)]))
        p_tg.append(tg_p)
        p_tv.append(tv_p)
        s_re.append(hre_s)
        s_im.append(him_s)
        s_c.append(c_pl)
        s_tg.append(tg_s)
        s_tv.append(tv_s)

    y_prompt = xp.reshape(nb, t, d)
    y_sample = xs.reshape(ts, ns, d).transpose(1, 0, 2)
    bps = t // tm_ffn
    p_tails = jnp.concatenate([jnp.stack(p_tg), jnp.stack(p_tv)], axis=2).reshape(depth, nb, bps, SUBLANES, 2 * dff)
    p_ff = p_tails[:, :, bps - 1, SUBLANES - (fk - 1):]
    s_cv = jnp.concatenate([cache_conv[:, :, ts:], jnp.stack(s_c).reshape(depth, ts, ns, cw).transpose(0, 2, 1, 3)],
                           axis=2)
    s_ff = jnp.concatenate([jnp.stack(s_tg), jnp.stack(s_tv)], axis=2)
    s_ff = s_ff.reshape(depth, fk - 1, ns, 2 * dff).transpose(0, 2, 1, 3)
    return (y_prompt, y_sample,
            jnp.stack(p_re).reshape(depth, nb, groups, nstate_g), jnp.stack(p_im).reshape(depth, nb, groups, nstate_g),
            jnp.stack(p_cv), p_ff,
            jnp.stack(s_re).reshape(depth, ns, groups, nstate_g), jnp.stack(s_im).reshape(depth, ns, groups, nstate_g),
            s_cv, s_ff)
```

```python
import functools

import jax
import jax.numpy as jnp
from jax import lax
from jax.experimental import pallas as pl
from jax.experimental.pallas import tpu as pltpu

F32 = jnp.float32
BF16 = jnp.bfloat16

LN_EPS = 1e-5
V7X_VMEM_LIMIT_BYTES = 56 * 1024 * 1024
SUBLANES = 8
LANES = 128
SUPER = 4
ROW_SPLIT = 2


def _params(sem):
    return pltpu.CompilerParams(dimension_semantics=sem, vmem_limit_bytes=V7X_VMEM_LIMIT_BYTES)


def _resident(stacked, l):
    shape = stacked.shape[1:]
    nd = len(shape)
    return pl.BlockSpec((None,) + shape, lambda *_: (l,) + (0,) * nd, pipeline_mode=pl.Buffered(1))


def _ln(r, g, b):
    mu = jnp.mean(r, axis=-1, keepdims=True)
    d = r - mu
    var = jnp.mean(d * d, axis=-1, keepdims=True)
    return d * lax.rsqrt(var + LN_EPS) * g + b


def _pick_rows(first_source, a_ref, b_ref, rows):
    return jnp.where(first_source, a_ref[rows, :], b_ref[rows, :])


def _in_proj_kernel(xa_ref, xb_ref, w_ref, u_ref, c_ref, *, s5w, cw, n_a):
    first = pl.program_id(0) < n_a
    rs = xa_ref.shape[0] // ROW_SPLIT
    for s in range(ROW_SPLIT):
        rows = slice(s * rs, (s + 1) * rs)
        xb = _pick_rows(first, xa_ref, xb_ref, rows).astype(BF16)
        u_ref[rows, :] = jnp.dot(xb, w_ref[:, :s5w], preferred_element_type=F32)
        cv = jnp.dot(xb, w_ref[:, s5w:s5w + cw], preferred_element_type=F32)
        cg = jnp.dot(xb, w_ref[:, s5w + cw:], preferred_element_type=F32)
        c_ref[rows, :] = cv * jax.nn.sigmoid(cg)


def _two_sources(n_a, b_blk, tm, width):
    return [pl.BlockSpec((tm, width), lambda i: (jnp.minimum(i, n_a - 1), 0)),
            pl.BlockSpec((tm, width), lambda i: (b_blk, 0))]


def _in_proj(xa, n_a, xb, b_blk, w_in_bf, l, s5w, cw, tm):
    d = xa.shape[1]
    n = (n_a + 1) * tm
    return pl.pallas_call(
        functools.partial(_in_proj_kernel, s5w=s5w, cw=cw, n_a=n_a),
        grid=(n_a + 1,),
        in_specs=_two_sources(n_a, b_blk, tm, d) + [_resident(w_in_bf, l)],
        out_specs=[pl.BlockSpec((tm, s5w), lambda i: (i, 0)), pl.BlockSpec((tm, cw), lambda i: (i, 0))],
        out_shape=[jax.ShapeDtypeStruct((n, s5w), F32), jax.ShapeDtypeStruct((n, cw), F32)],
        compiler_params=_params(("arbitrary",)),
        name="in_proj",
    )(xa, xb, w_in_bf)


def _s5_seq_kernel(u_ref, wb_ref, wc_ref, are_ref, aim_ref, d_ref, h0re_ref, h0im_ref,
                   y_ref, hre_ref, him_ref, sre, sim, *, tc, sgw):
    step_rows = SUBLANES
    k = pl.program_id(1)

    @pl.when(k == 0)
    def _():
        hre_ref[...] = h0re_ref[...]
        him_ref[...] = h0im_ref[...]

    u = u_ref[...]
    ub = u.astype(BF16)
    for v in range(SUPER):
        r = jnp.dot(ub[:, v * sgw:(v + 1) * sgw], wb_ref[v], preferred_element_type=F32)
        for i in range(SUBLANES):
            sre[v, pl.ds(i, tc, stride=step_rows), :] = r[:, i * LANES:(i + 1) * LANES]
            sim[v, pl.ds(i, tc, stride=step_rows), :] = r[:, (SUBLANES + i) * LANES:(SUBLANES + i + 1) * LANES]

    a_re = [are_ref[v * SUBLANES:(v + 1) * SUBLANES, :] for v in range(SUPER)]
    a_im = [aim_ref[v * SUBLANES:(v + 1) * SUBLANES, :] for v in range(SUPER)]
    init = tuple(hre_ref[0, v * SUBLANES:(v + 1) * SUBLANES, :] for v in range(SUPER)) + \
        tuple(him_ref[0, v * SUBLANES:(v + 1) * SUBLANES, :] for v in range(SUPER))

    def step(t, carry):
        row = pl.multiple_of(t * step_rows, step_rows)
        new_re, new_im = [], []
        for v in range(SUPER):
            hr, hi = carry[v], carry[SUPER + v]
            nr = a_re[v] * hr - a_im[v] * hi + sre[v, pl.ds(row, step_rows), :]
            ni = a_re[v] * hi + a_im[v] * hr + sim[v, pl.ds(row, step_rows), :]
            sre[v, pl.ds(row, step_rows), :] = nr
            sim[v, pl.ds(row, step_rows), :] = ni
            new_re.append(nr)
            new_im.append(ni)
        return tuple(new_re) + tuple(new_im)

    fin = lax.fori_loop(0, tc, step, init, unroll=8)
    for v in range(SUPER):
        hre_ref[0, v * SUBLANES:(v + 1) * SUBLANES, :] = fin[v]
        him_ref[0, v * SUBLANES:(v + 1) * SUBLANES, :] = fin[SUPER + v]

    for v in range(SUPER):
        cols = [sre[v, pl.ds(i, tc, stride=step_rows), :] for i in range(SUBLANES)]
        cols += [sim[v, pl.ds(i, tc, stride=step_rows), :] for i in range(SUBLANES)]
        hcat = jnp.concatenate(cols, axis=1).astype(BF16)
        y = jnp.dot(hcat, wc_ref[v], preferred_element_type=F32)
        y_ref[:, v * sgw:(v + 1) * sgw] = y + d_ref[:, v * sgw:(v + 1) * sgw] * u[:, v * sgw:(v + 1) * sgw]


def _s5_seq(u, wb, wc, a_re, a_im, d, h0re, h0im, l, nb, t, tc):
    s5w = u.shape[1]
    sgw = s5w // SUPER
    nk = t // tc
    rows = a_re.shape[1]
    st_spec = pl.BlockSpec((1, rows, LANES), lambda b, k: (b, 0, 0))
    return pl.pallas_call(
        functools.partial(_s5_seq_kernel, tc=tc, sgw=sgw),
        grid=(nb, nk),
        in_specs=[pl.BlockSpec((tc, s5w), lambda b, k: (b * nk + k, 0)),
                  _resident(wb, l), _resident(wc, l), _resident(a_re, l), _resident(a_im, l),
                  _resident(d, l), st_spec, st_spec],
        out_specs=[pl.BlockSpec((tc, s5w), lambda b, k: (b * nk + k, 0)), st_spec, st_spec],
        out_shape=[jax.ShapeDtypeStruct((nb * t, s5w), F32),
                   jax.ShapeDtypeStruct((nb, rows, LANES), F32), jax.ShapeDtypeStruct((nb, rows, LANES), F32)],
        scratch_shapes=[pltpu.VMEM((SUPER, tc * SUBLANES, LANES), F32),
                        pltpu.VMEM((SUPER, tc * SUBLANES, LANES), F32)],
        compiler_params=_params(("arbitrary", "arbitrary")),
        name="s5_seq",
    )(u, wb, wc, a_re, a_im, d, h0re, h0im)


def _s5_planes_kernel(u_ref, wb_ref, wc_ref, are_ref, aim_ref, d_ref, h0re_ref, h0im_ref,
                      y_ref, hre_ref, him_ref, *, nt, nseq, half):
    u = u_ref[...]
    r = jnp.dot(u.astype(BF16), wb_ref[0], preferred_element_type=F32)
    a_re, a_im = are_ref[...], aim_ref[...]
    hr, hi = h0re_ref[...], h0im_ref[...]
    hs_re, hs_im = [], []
    for t in range(nt):
        br = r[t * nseq:(t + 1) * nseq, :half]
        bi = r[t * nseq:(t + 1) * nseq, half:]
        hr, hi = a_re * hr - a_im * hi + br, a_re * hi + a_im * hr + bi
        hs_re.append(hr)
        hs_im.append(hi)
    hre_ref[...] = hr
    him_ref[...] = hi
    hcat = jnp.concatenate([jnp.concatenate(hs_re, axis=0), jnp.concatenate(hs_im, axis=0)], axis=1)
    y = jnp.dot(hcat.astype(BF16), wc_ref[0], preferred_element_type=F32)
    y_ref[...] = y + d_ref[...] * u


def _s5_planes(u, u_blk, wb, wc, a_re, a_im, d, h0re, h0im, l, nt, nseq):
    n, s5w = nt * nseq, u.shape[1]
    sgw = s5w // SUPER
    nstate = h0re.shape[2]
    half = nstate // SUPER
    return pl.pallas_call(
        functools.partial(_s5_planes_kernel, nt=nt, nseq=nseq, half=half),
        grid=(SUPER,),
        in_specs=[pl.BlockSpec((n, sgw), lambda v: (u_blk, v)),
                  pl.BlockSpec((None, 1) + wb.shape[2:], lambda v: (l, v, 0, 0)),
                  pl.BlockSpec((None, 1) + wc.shape[2:], lambda v: (l, v, 0, 0)),
                  pl.BlockSpec((None, 1, half), lambda v: (l, 0, v)),
                  pl.BlockSpec((None, 1, half), lambda v: (l, 0, v)),
                  pl.BlockSpec((None, 1, sgw), lambda v: (l, 0, v)),
                  pl.BlockSpec((None, nseq, half), lambda v: (l, 0, v)),
                  pl.BlockSpec((None, nseq, half), lambda v: (l, 0, v))],
        out_specs=[pl.BlockSpec((n, sgw), lambda v: (0, v)),
                   pl.BlockSpec((nseq, half), lambda v: (0, v)), pl.BlockSpec((nseq, half), lambda v: (0, v))],
        out_shape=[jax.ShapeDtypeStruct((n, s5w), F32),
                   jax.ShapeDtypeStruct((nseq, nstate), F32), jax.ShapeDtypeStruct((nseq, nstate), F32)],
        compiler_params=_params(("arbitrary",)),
        name="s5_planes",
    )(u, wb, wc, a_re, a_im, d, h0re, h0im)


CONV_ROW_TILE = 32
CONV_COL_TILE = 512


def _conv_seq_kernel(c_ref, halo_ref, w_ref, b_ref, g_ref, bt_ref, out_ref, xp_ref, sh_ref, y_ref,
                     *, tm, kk, halo, blocks_per_seq):
    i = pl.program_id(0)
    start = (i % blocks_per_seq) == 0
    ncol = c_ref.shape[1]
    xp_ref[0:halo, 0:ncol] = jnp.where(start, 0.0, halo_ref[...])
    xp_ref[halo:halo + tm, 0:ncol] = c_ref[...]
    ncopy = tm + halo - SUBLANES
    for b in range(1, SUBLANES):
        sh_ref[b - 1, :, 0:ncol] = xp_ref[pl.ds(b, ncopy), 0:ncol]
    rt = CONV_ROW_TILE
    ngrp = rt // SUBLANES
    for lc in range(ncol // CONV_COL_TILE):
        cols = slice(lc * CONV_COL_TILE, (lc + 1) * CONV_COL_TILE)

        def tile(r, carry, cols=cols):
            r0 = pl.multiple_of(r * rt, rt)
            acc = [jnp.broadcast_to(b_ref[:, cols], (SUBLANES, CONV_COL_TILE))] * ngrp
            for j in range(kk):
                a, b = divmod(halo - j, SUBLANES)
                k = kk - 1 - j
                w8 = w_ref[SUBLANES * k:SUBLANES * (k + 1), cols]
                for q in range(ngrp):
                    rows = pl.ds(r0 + SUBLANES * (a + q), SUBLANES)
                    xs = xp_ref[rows, cols] if b == 0 else sh_ref[b - 1, rows, cols]
                    acc[q] = acc[q] + w8 * xs
            for q in range(ngrp):
                y_ref[pl.ds(r0 + SUBLANES * q, SUBLANES), cols] = acc[q]
            return carry

        lax.fori_loop(0, tm // rt, tile, 0)
    out_ref[...] = jax.nn.silu(_ln(y_ref[...], g_ref[...], bt_ref[...]))


def _conv_seq(c, n, w, b, g, bt, l, t, tm, kk):
    cw = c.shape[1]
    halo = 32
    assert kk - 1 <= halo and tm % halo == 0 and t % tm == 0
    hb = tm // halo
    return pl.pallas_call(
        functools.partial(_conv_seq_kernel, tm=tm, kk=kk, halo=halo, blocks_per_seq=t // tm),
        grid=(n // tm,),
        in_specs=[pl.BlockSpec((tm, cw), lambda i: (i, 0)),
                  pl.BlockSpec((halo, cw), lambda i: (jnp.maximum(i * hb - 1, 0), 0)),
                  _resident(w, l), _resident(b, l), _resident(g, l), _resident(bt, l)],
        out_specs=pl.BlockSpec((tm, cw), lambda i: (i, 0)),
        out_shape=jax.ShapeDtypeStruct((n, cw), F32),
        scratch_shapes=[pltpu.VMEM((tm + halo, cw + LANES), F32),
                        pltpu.VMEM((SUBLANES - 1, tm + halo - SUBLANES, cw + LANES), F32),
                        pltpu.VMEM((tm, cw), F32)],
        compiler_params=_params(("arbitrary",)),
        name="conv_seq",
    )(c, c, w, b, g, bt)


def _conv_planes_kernel(cache_ref, c_ref, w_ref, b_ref, g_ref, bt_ref, out_ref, *, nt, kk):
    rows, cw = c_ref.shape[1], c_ref.shape[2]
    gam, bet = g_ref[...], bt_ref[...]
    for t in range(nt):
        acc = jnp.broadcast_to(b_ref[...], (rows, cw))
        for k in range(kk):
            p = t + k
            xs = cache_ref[p] if p < kk - 1 else c_ref[p - (kk - 1)]
            w8 = w_ref[SUBLANES * k:SUBLANES * (k + 1), :]
            acc = acc + jnp.concatenate([w8] * (rows // SUBLANES), axis=0) * xs
        out_ref[t] = jax.nn.silu(_ln(acc, gam, bet))


def _conv_planes(cache_t, c_t, w8, b, g, bt, l, kk, rows):
    nt, nseq, cw = c_t.shape
    return pl.pallas_call(
        functools.partial(_conv_planes_kernel, nt=nt, kk=kk),
        grid=(nseq // rows,),
        in_specs=[pl.BlockSpec((None, kk - 1, rows, cw), lambda i: (l, 0, i, 0)),
                  pl.BlockSpec((nt, rows, cw), lambda i: (0, i, 0)),
                  _resident(w8, l), _resident(b, l), _resident(g, l), _resident(bt, l)],
        out_specs=pl.BlockSpec((nt, rows, cw), lambda i: (0, i, 0)),
        out_shape=jax.ShapeDtypeStruct((nt, nseq, cw), F32),
        compiler_params=_params(("arbitrary",)),
        name="conv_planes",
    )(cache_t, c_t, w8, b, g, bt)


def _mix_out_kernel(y5a_ref, y5b_ref, caa_ref, cab_ref, xa_ref, xb_ref, wglu_ref, wout_ref, g_ref, b_ref, out_ref,
                    *, alpha, s5w, n_a):
    first = pl.program_id(0) < n_a
    rs = xa_ref.shape[0] // ROW_SPLIT
    for s in range(ROW_SPLIT):
        rows = slice(s * rs, (s + 1) * rs)
        g5 = jax.nn.gelu(_pick_rows(first, y5a_ref, y5b_ref, rows))
        gate = jnp.dot(g5.astype(BF16), wglu_ref[...], preferred_element_type=F32)
        s5o = g5 * jax.nn.sigmoid(gate)
        mix = jnp.dot(s5o.astype(BF16), wout_ref[:s5w, :], preferred_element_type=F32)
        ca = _pick_rows(first, caa_ref, cab_ref, rows)
        mix = mix + jnp.dot(ca.astype(BF16), wout_ref[s5w:, :], preferred_element_type=F32)
        x = _pick_rows(first, xa_ref, xb_ref, rows)
        out_ref[rows, :] = _ln(alpha * x + mix, g_ref[...], b_ref[...])


def _mix_out(y5a, y5b, caa, cab, xa, n_a, xb, b_blk, wglu, wout, g, b, l, alpha, tm):
    d = xa.shape[1]
    s5w, cw = y5a.shape[1], caa.shape[1]
    n = (n_a + 1) * tm
    return pl.pallas_call(
        functools.partial(_mix_out_kernel, alpha=alpha, s5w=s5w, n_a=n_a),
        grid=(n_a + 1,),
        in_specs=_two_sources(n_a, 0, tm, s5w) + _two_sources(n_a, 0, tm, cw) + _two_sources(n_a, b_blk, tm, d) +
        [_resident(wglu, l), _resident(wout, l), _resident(g, l), _resident(b, l)],
        out_specs=pl.BlockSpec((tm, d), lambda i: (i, 0)),
        out_shape=jax.ShapeDtypeStruct((n, d), F32),
        compiler_params=_params(("arbitrary",)),
        name="mix_out",
    )(y5a, y5b, caa, cab, xa, xb, wglu, wout, g, b)


FFN_HALO = 16


def _ffn_kernel(*refs, alpha, tm, planes, nseq, blocks_per_seq):
    if planes:
        (x_ref, hg_c_ref, hv_c_ref, wg_ref, wv_ref, wd_ref, cwg_ref, cwv_ref, cbg_ref, cbv_ref, g_ref, b_ref,
         out_ref, tg_ref, tv_ref, xb_ref) = refs
    else:
        (x_ref, halo_ref, wg_ref, wv_ref, wd_ref, cwg_ref, cwv_ref, cbg_ref, cbv_ref, g_ref, b_ref,
         out_ref, tg_ref, tv_ref, xb_ref) = refs
    i = pl.program_id(0)
    j = pl.program_id(1)

    @pl.when(j == 0)
    def _():
        if planes:
            xb_ref[...] = x_ref[...].astype(BF16)
        else:
            start = (i % blocks_per_seq) == 0
            xb_ref[0:FFN_HALO, :] = jnp.where(start, 0.0, halo_ref[...]).astype(BF16)
            xb_ref[FFN_HALO:, :] = x_ref[...].astype(BF16)
        out_ref[...] = jnp.zeros_like(out_ref)

    xb = xb_ref[...]
    hg = jnp.dot(xb, wg_ref[...], preferred_element_type=F32)
    hv = jnp.dot(xb, wv_ref[...], preferred_element_type=F32)

    def conv3(h, hist_ref, cw_ref, cb_ref, tail_ref):
        w0, w1, w2 = cw_ref[0:1, :], cw_ref[1:2, :], cw_ref[2:3, :]
        if planes:
            hp = jnp.concatenate([hist_ref[...], h], axis=0)
            cur, prev1, prev2 = hp[2 * nseq:], hp[nseq:nseq + tm], hp[:tm]
            tail_ref[...] = h[tm - 2 * nseq:, :]
        else:
            cur = h[FFN_HALO:, :]
            prev1 = h[FFN_HALO - 1:FFN_HALO - 1 + tm, :]
            prev2 = h[FFN_HALO - 2:FFN_HALO - 2 + tm, :]
            tail_ref[...] = h[FFN_HALO + tm - SUBLANES:, :]
        return w2 * cur + w1 * prev1 + w0 * prev2 + cb_ref[...]

    cg = conv3(hg, hg_c_ref if planes else None, cwg_ref, cbg_ref, tg_ref)
    cv = conv3(hv, hv_c_ref if planes else None, cwv_ref, cbv_ref, tv_ref)
    act = (jax.nn.silu(cg) * cv).astype(BF16)
    out_ref[...] += jnp.dot(act, wd_ref[...], preferred_element_type=F32)

    @pl.when(j == pl.num_programs(1) - 1)
    def _():
        out_ref[...] = _ln(alpha * x_ref[...] + out_ref[...], g_ref[...], b_ref[...])


def _ffn(x, n, w_up, w_down, conv_w, conv_b, g, b, l, alpha, tm, tf, *, cache=None, x_blk=0, t=None, nseq=None):
    d = x.shape[1]
    dff = w_down.shape[1]
    nj = dff // tf
    fk = conv_w.shape[1]
    planes = cache is not None
    wspecs = [pl.BlockSpec((None, d, tf), lambda i, j: (l, 0, j)),
              pl.BlockSpec((None, d, tf), lambda i, j: (l, 0, nj + j)),
              pl.BlockSpec((None, tf, d), lambda i, j: (l, j, 0)),
              pl.BlockSpec((None, fk, tf), lambda i, j: (l, 0, j)),
              pl.BlockSpec((None, fk, tf), lambda i, j: (l, 0, nj + j)),
              pl.BlockSpec((None, 1, tf), lambda i, j: (l, 0, j)),
              pl.BlockSpec((None, 1, tf), lambda i, j: (l, 0, nj + j)),
              pl.BlockSpec((None, 1, d), lambda i, j: (l, 0, 0)), pl.BlockSpec((None, 1, d), lambda i, j: (l, 0, 0))]
    wargs = [w_up, w_up, w_down, conv_w, conv_w, conv_b, conv_b, g, b]
    if planes:
        assert n == tm
        hist = cache.shape[1]
        in_specs = [pl.BlockSpec((tm, d), lambda i, j: (x_blk, 0)),
                    pl.BlockSpec((None, hist, tf), lambda i, j: (l, 0, j)),
                    pl.BlockSpec((None, hist, tf), lambda i, j: (l, 0, nj + j))] + wspecs
        args = [x, cache, cache] + wargs
        tail_rows, xb_rows, bps = hist, tm, 1
    else:
        hb = tm // FFN_HALO
        in_specs = [pl.BlockSpec((tm, d), lambda i, j: (i, 0), pipeline_mode=pl.Buffered(1)),
                    pl.BlockSpec((FFN_HALO, d), lambda i, j: (jnp.maximum(i * hb - 1, 0), 0))] + wspecs
        args = [x, x] + wargs
        tail_rows, xb_rows, bps = SUBLANES, tm + FFN_HALO, t // tm
    nblk = n // tm
    tail_spec = pl.BlockSpec((tail_rows, tf), lambda i, j: (i, j))
    return pl.pallas_call(
        functools.partial(_ffn_kernel, alpha=alpha, tm=tm, planes=planes, nseq=nseq, blocks_per_seq=bps),
        grid=(nblk, nj),
        in_specs=in_specs,
        out_specs=[pl.BlockSpec((tm, d), lambda i, j: (i, 0)), tail_spec, tail_spec],
        out_shape=[jax.ShapeDtypeStruct((n, d), F32),
                   jax.ShapeDtypeStruct((nblk * tail_rows, dff), F32),
                   jax.ShapeDtypeStruct((nblk * tail_rows, dff), F32)],
        scratch_shapes=[pltpu.VMEM((xb_rows, d), BF16)],
        compiler_params=_params(("arbitrary", "arbitrary")),
        name="ffn_planes" if planes else "ffn_seq",
    )(*args)


def _pe_kernel(x_ref, p_ref, wg_ref, wp_ref, g_ref, b_ref, out_ref, *, alpha):
    rs = x_ref.shape[0] // ROW_SPLIT
    for s in range(ROW_SPLIT):
        rows = slice(s * rs, (s + 1) * rs)
        x = x_ref[rows, :]
        gate = jax.nn.sigmoid(jnp.dot(x.astype(BF16), wg_ref[...], preferred_element_type=F32))
        e = jnp.dot(p_ref[rows, :].astype(BF16), wp_ref[...], preferred_element_type=F32)
        out_ref[rows, :] = _ln(alpha * x + gate * e, g_ref[...], b_ref[...])


def _pe(x, p, wg, wp, g, b, l, alpha, tm):
    n, d = x.shape
    pd = p.shape[2]
    return pl.pallas_call(
        functools.partial(_pe_kernel, alpha=alpha),
        grid=(n // tm,),
        in_specs=[pl.BlockSpec((tm, d), lambda i: (i, 0)), pl.BlockSpec((None, tm, pd), lambda i: (l, i, 0)),
                  _resident(wg, l), _resident(wp, l), _resident(g, l), _resident(b, l)],
        out_specs=pl.BlockSpec((tm, d), lambda i: (i, 0)),
        out_shape=jax.ShapeDtypeStruct((n, d), F32),
        compiler_params=_params(("arbitrary",)),
        name="pe_embed",
    )(x, p, wg, wp, g, b)


def _pe2_kernel(xa_ref, xb_ref, pa_ref, pb_ref, wg_ref, wp_ref, g_ref, b_ref, out_ref, *, alpha, n_a):
    first = pl.program_id(0) < n_a
    rs = xa_ref.shape[0] // ROW_SPLIT
    for s in range(ROW_SPLIT):
        rows = slice(s * rs, (s + 1) * rs)
        x = _pick_rows(first, xa_ref, xb_ref, rows)
        gate = jax.nn.sigmoid(jnp.dot(x.astype(BF16), wg_ref[...], preferred_element_type=F32))
        e = jnp.dot(_pick_rows(first, pa_ref, pb_ref, rows).astype(BF16), wp_ref[...], preferred_element_type=F32)
        out_ref[rows, :] = _ln(alpha * x + gate * e, g_ref[...], b_ref[...])


def _pe2(xa, xb, pa, pb, wg, wp, g, b, l, alpha, tm):
    d = xa.shape[1]
    pd = pa.shape[2]
    n_a = xa.shape[0] // tm
    n = (n_a + 1) * tm
    return pl.pallas_call(
        functools.partial(_pe2_kernel, alpha=alpha, n_a=n_a),
        grid=(n_a + 1,),
        in_specs=_two_sources(n_a, 0, tm, d) +
        [pl.BlockSpec((None, tm, pd), lambda i: (l, jnp.minimum(i, n_a - 1), 0)),
         pl.BlockSpec((None, tm, pd), lambda i: (l, 0, 0)),
         _resident(wg, l), _resident(wp, l), _resident(g, l), _resident(b, l)],
        out_specs=pl.BlockSpec((tm, d), lambda i: (i, 0)),
        out_shape=jax.ShapeDtypeStruct((n, d), F32),
        compiler_params=_params(("arbitrary",)),
        name="pe_embed2",
    )(xa, xb, pa, pb, wg, wp, g, b)


def _s5_discretise(lam_re, lam_im, log_dt, b_re, b_im):
    lr, li = lam_re.astype(F32), lam_im.astype(F32)
    dt = jnp.exp(log_dt.astype(F32))[..., None]
    mag = jnp.exp(lr * dt)
    ab_re = mag * jnp.cos(li * dt)
    ab_im = mag * jnp.sin(li * dt)
    num_re, num_im = ab_re - 1.0, ab_im
    den = lr * lr + li * li
    q_re = (num_re * lr + num_im * li) / den
    q_im = (num_im * lr - num_re * li) / den
    br, bi = b_re.astype(F32), b_im.astype(F32)
    bb_re = q_re[..., None] * br - q_im[..., None] * bi
    bb_im = q_re[..., None] * bi + q_im[..., None] * br
    return ab_re, ab_im, bb_re, bb_im


def _blockdiag_kernel(tbr_ref, tbi_ref, tcr_ref, tci_ref, eb_ref, ec_ref, wb_ref, wc_ref, *, gs, h, p):
    def place(t_ref, e_ref, rows_per_block, cols_per_block):
        tiled = jnp.dot(t_ref[...].astype(BF16), e_ref[...], preferred_element_type=F32)
        rg = lax.broadcasted_iota(jnp.int32, tiled.shape, 0) // rows_per_block
        cg = lax.broadcasted_iota(jnp.int32, tiled.shape, 1) // cols_per_block
        return jnp.where(rg == cg, tiled, 0.0).astype(BF16)

    wb_ref[:, :gs * p] = place(tbr_ref, eb_ref, h, p)
    wb_ref[:, gs * p:] = place(tbi_ref, eb_ref, h, p)
    wc_ref[:gs * p, :] = place(tcr_ref, ec_ref, p, h)
    wc_ref[gs * p:, :] = place(tci_ref, ec_ref, p, h)


def _s5_block_weights(bb_re, bb_im, c_re, c_im):
    nl, g, p, h = bb_re.shape
    gs = g // SUPER
    n = nl * SUPER
    tb = [v.transpose(0, 1, 3, 2).reshape(n, gs * h, p) for v in (bb_re, bb_im)]
    tc = [v.astype(F32).transpose(0, 1, 3, 2).reshape(n, gs * p, h) for v in (c_re, -c_im)]
    eb = jnp.tile(jnp.eye(p, dtype=BF16), (1, gs))
    ec = jnp.tile(jnp.eye(h, dtype=BF16), (1, gs))
    blk = lambda a: pl.BlockSpec((None,) + a.shape[1:], lambda i: (i, 0, 0))
    whole = lambda a: pl.BlockSpec(a.shape, lambda i: (0, 0))
    wb, wc = pl.pallas_call(
        functools.partial(_blockdiag_kernel, gs=gs, h=h, p=p),
        grid=(n,),
        in_specs=[blk(tb[0]), blk(tb[1]), blk(tc[0]), blk(tc[1]), whole(eb), whole(ec)],
        out_specs=[pl.BlockSpec((None, gs * h, 2 * gs * p), lambda i: (i, 0, 0)),
                   pl.BlockSpec((None, 2 * gs * p, gs * h), lambda i: (i, 0, 0))],
        out_shape=[jax.ShapeDtypeStruct((n, gs * h, 2 * gs * p), BF16),
                   jax.ShapeDtypeStruct((n, 2 * gs * p, gs * h), BF16)],
        compiler_params=_params(("arbitrary",)),
        name="s5_blockdiag",
    )(tb[0], tb[1], tc[0], tc[1], eb, ec)
    return wb.reshape(nl, SUPER, gs * h, 2 * gs * p), wc.reshape(nl, SUPER, 2 * gs * p, gs * h)


def _rows(v):
    return v.astype(F32).reshape(v.shape[0], 1, -1)


def kernel(x_prompt, x_sample, state_s5_re, state_s5_im, cache_conv, cache_ffn_conv, p_prompt, p_sample,
           w_in, s5_lam_re, s5_lam_im, s5_log_dt, s5_b_re, s5_b_im, s5_c_re, s5_c_im, s5_d, s5_w_glu,
           conv_w, conv_b, conv_ln_g, conv_ln_b, w_out, ln1_g, ln1_b,
           ffn_w_up, ffn_conv_w, ffn_conv_b, ffn_w_down, ln2_g, ln2_b,
           pe_w, pe_w_gate, ln3_g, ln3_b):
    nb, t, d = x_prompt.shape
    ns, ts, _ = x_sample.shape
    depth = w_in.shape[0]
    groups, nstate_g = s5_lam_re.shape[1], s5_lam_re.shape[2]
    s5w = groups * s5_d.shape[2]
    cw = d - s5w
    nstate = groups * nstate_g
    kk = conv_w.shape[1]
    fk = ffn_conv_w.shape[1]
    dff = ffn_w_down.shape[1]
    assert fk == 3 and ts >= fk - 1 and ts <= kk - 1 and t >= kk - 1
    alpha = (2.0 * depth) ** 0.25
    tm, tm_ffn, tf, tc, tconv = 512, 1024, 512, 512, 512

    xp = x_prompt.reshape(nb * t, d)
    xs = x_sample.transpose(1, 0, 2).reshape(ts * ns, d)
    zero_state = jnp.zeros((nb, nstate // LANES, LANES), F32)

    w_in_bf, wglu_bf, wout_bf = w_in.astype(BF16), s5_w_glu.astype(BF16), w_out.astype(BF16)
    wup_bf, wdown_bf = ffn_w_up.astype(BF16), ffn_w_down.astype(BF16)
    wpe_bf, wgate_bf = pe_w.astype(BF16), pe_w_gate.astype(BF16)
    ab_re, ab_im, bb_re, bb_im = _s5_discretise(s5_lam_re, s5_lam_im, s5_log_dt, s5_b_re, s5_b_im)
    wb, wc = _s5_block_weights(bb_re, bb_im, s5_c_re, s5_c_im)
    a_tile_re, a_tile_im = (v.reshape(depth, nstate // LANES, LANES) for v in (ab_re, ab_im))
    a_row_re, a_row_im = (v.reshape(depth, 1, nstate) for v in (ab_re, ab_im))
    d_rows = _rows(s5_d.reshape(depth, s5w))
    conv_w1 = conv_w.astype(F32)
    conv_w8 = jnp.repeat(conv_w1, SUBLANES, axis=1)
    conv_b_rows, cg_rows, cb_rows = _rows(conv_b), _rows(conv_ln_g), _rows(conv_ln_b)
    fcw, fcb = ffn_conv_w.astype(F32), _rows(ffn_conv_b)
    l1g, l1b, l2g, l2b, l3g, l3b = (_rows(v) for v in (ln1_g, ln1_b, ln2_g, ln2_b, ln3_g, ln3_b))
    pp = p_prompt.reshape(depth, nb * t, -1)
    ps = p_sample.transpose(0, 2, 1, 3).reshape(depth, ts * ns, -1)
    h0_re = state_s5_re.reshape(depth, ns, nstate).astype(F32)
    h0_im = state_s5_im.reshape(depth, ns, nstate).astype(F32)
    cache_t = cache_conv.transpose(0, 2, 1, 3)
    fcache = cache_ffn_conv.transpose(0, 2, 1, 3).reshape(depth, (fk - 1) * ns, 2 * dff)

    p_re, p_im, p_cv, p_tg, p_tv = [], [], [], [], []
    s_re, s_im, s_c, s_tg, s_tv = [], [], [], [], []
    n_p, n_s = nb * t, ts * ns
    assert n_s == tm and n_p % tm == 0
    nblk_p = n_p // tm
    x_seq, x_pl, x_pl_blk = xp, xs, 0
    for l in range(depth):
        u, c = _in_proj(x_seq, nblk_p, x_pl, x_pl_blk, w_in_bf, l, s5w, cw, tm)
        c_pl = c[n_p:]
        y5p, hre_p, him_p = _s5_seq(u, wb, wc, a_tile_re, a_tile_im, d_rows, zero_state, zero_state, l, nb, t, tc)
        y5s, hre_s, him_s = _s5_planes(u, nblk_p, wb, wc, a_row_re, a_row_im, d_rows, h0_re, h0_im, l, ts, ns)
        cap = _conv_seq(c, n_p, conv_w8, conv_b_rows, cg_rows, cb_rows, l, t, tconv, kk)
        cas = _conv_planes(cache_t, c_pl.reshape(ts, ns, cw), conv_w8, conv_b_rows, cg_rows, cb_rows, l, kk, 32)
        x1 = _mix_out(y5p, y5s, cap, cas.reshape(n_s, cw), x_seq, nblk_p, x_pl, x_pl_blk,
                      wglu_bf, wout_bf, l1g, l1b, l, alpha, tm)
        x2p, tg_p, tv_p = _ffn(x1, n_p, wup_bf, wdown_bf, fcw, fcb, l2g, l2b, l, alpha, tm_ffn, tf, t=t)
        x2s, tg_s, tv_s = _ffn(x1, n_s, wup_bf, wdown_bf, fcw, fcb, l2g, l2b, l, alpha, n_s, tf,
                               cache=fcache, x_blk=nblk_p, nseq=ns)
        if l + 1 < depth:
            x_all = _pe2(x2p, x2s, pp, ps, wgate_bf, wpe_bf, l3g, l3b, l, alpha, tm)
            x_seq, x_pl, x_pl_blk = x_all, x_all, nblk_p
        else:
            xp = _pe(x2p, pp, wgate_bf, wpe_bf, l3g, l3b, l, alpha, tm)
            xs = _pe(x2s, ps, wgate_bf, wpe_bf, l3g, l3b, l, alpha, tm)
        p_re.append(hre_p)
        p_im.append(him_p)
        p_cv.append(jnp.stack([c[(b + 1) * t - (kk - 1):(b + 1) * t] for b in range(nb)]))
        p_tg.append(tg_p)
        p_tv.append(tv_p)
        s_re.append(hre_s)
        s_im.append(him_s)
        s_c.append(c_pl)
        s_tg.append(tg_s)
        s_tv.append(tv_s)

    y_prompt = xp.reshape(nb, t, d)
    y_sample = xs.reshape(ts, ns, d).transpose(1, 0, 2)
    bps = t // tm_ffn
    p_tails = jnp.concatenate([jnp.stack(p_tg), jnp.stack(p_tv)], axis=2).reshape(depth, nb, bps, SUBLANES, 2 * dff)
    p_ff = p_tails[:, :, bps - 1, SUBLANES - (fk - 1):]
    s_cv = jnp.concatenate([cache_conv[:, :, ts:], jnp.stack(s_c).reshape(depth, ts, ns, cw).transpose(0, 2, 1, 3)],
                           axis=2)
    s_ff = jnp.concatenate([jnp.stack(s_tg), jnp.stack(s_tv)], axis=2)
    s_ff = s_ff.reshape(depth, fk - 1, ns, 2 * dff).transpose(0, 2, 1, 3)
    return (y_prompt, y_sample,
            jnp.stack(p_re).reshape(depth, nb, groups, nstate_g), jnp.stack(p_im).reshape(depth, nb, groups, nstate_g),
            jnp.stack(p_cv), p_ff,
            jnp.stack(s_re).reshape(depth, ns, groups, nstate_g), jnp.stack(s_im).reshape(depth, ns, groups, nstate_g),
            s_cv, s_ff)
```

```python
import functools

import jax
import jax.numpy as jnp
from jax import lax
from jax.experimental import pallas as pl
from jax.experimental.pallas import tpu as pltpu

F32 = jnp.float32
BF16 = jnp.bfloat16

LN_EPS = 1e-5
V7X_VMEM_LIMIT_BYTES = 56 * 1024 * 1024
SUBLANES = 8
LANES = 128
SUPER = 4
ROW_SPLIT = 2


def _params(sem):
    return pltpu.CompilerParams(dimension_semantics=sem, vmem_limit_bytes=V7X_VMEM_LIMIT_BYTES)


def _resident(stacked, l):
    shape = stacked.shape[1:]
    nd = len(shape)
    return pl.BlockSpec((None,) + shape, lambda *_: (l,) + (0,) * nd, pipeline_mode=pl.Buffered(1))


def _ln(r, g, b):
    mu = jnp.mean(r, axis=-1, keepdims=True)
    d = r - mu
    var = jnp.mean(d * d, axis=-1, keepdims=True)
    return d * lax.rsqrt(var + LN_EPS) * g + b


def _pick_rows(first_source, a_ref, b_ref, rows):
    return jnp.where(first_source, a_ref[rows, :], b_ref[rows, :])


def _in_proj_kernel(xa_ref, xb_ref, w_ref, u_ref, c_ref, *, s5w, cw, n_a):
    first = pl.program_id(0) < n_a
    rs = xa_ref.shape[0] // ROW_SPLIT
    for s in range(ROW_SPLIT):
        rows = slice(s * rs, (s + 1) * rs)
        xb = _pick_rows(first, xa_ref, xb_ref, rows).astype(BF16)
        u_ref[rows, :] = jnp.dot(xb, w_ref[:, :s5w], preferred_element_type=F32)
        cv = jnp.dot(xb, w_ref[:, s5w:s5w + cw], preferred_element_type=F32)
        cg = jnp.dot(xb, w_ref[:, s5w + cw:], preferred_element_type=F32)
        c_ref[rows, :] = cv * jax.nn.sigmoid(cg)


def _two_sources(n_a, b_blk, tm, width):
    return [pl.BlockSpec((tm, width), lambda i: (jnp.minimum(i, n_a - 1), 0)),
            pl.BlockSpec((tm, width), lambda i: (b_blk, 0))]


def _in_proj(xa, n_a, xb, b_blk, w_in_bf, l, s5w, cw, tm):
    d = xa.shape[1]
    n = (n_a + 1) * tm
    return pl.pallas_call(
        functools.partial(_in_proj_kernel, s5w=s5w, cw=cw, n_a=n_a),
        grid=(n_a + 1,),
        in_specs=_two_sources(n_a, b_blk, tm, d) + [_resident(w_in_bf, l)],
        out_specs=[pl.BlockSpec((tm, s5w), lambda i: (i, 0)), pl.BlockSpec((tm, cw), lambda i: (i, 0))],
        out_shape=[jax.ShapeDtypeStruct((n, s5w), F32), jax.ShapeDtypeStruct((n, cw), F32)],
        compiler_params=_params(("arbitrary",)),
        name="in_proj",
    )(xa, xb, w_in_bf)


def _s5_seq_kernel(u_ref, wb_ref, wc_ref, are_ref, aim_ref, d_ref, h0re_ref, h0im_ref,
                   y_ref, hre_ref, him_ref, sre, sim, *, tc, sgw):
    step_rows = SUBLANES
    k = pl.program_id(1)

    @pl.when(k == 0)
    def _():
        hre_ref[...] = h0re_ref[...]
        him_ref[...] = h0im_ref[...]

    u = u_ref[...]
    ub = u.astype(BF16)
    for v in range(SUPER):
        r = jnp.dot(ub[:, v * sgw:(v + 1) * sgw], wb_ref[v], preferred_element_type=F32)
        for i in range(SUBLANES):
            sre[v, pl.ds(i, tc, stride=step_rows), :] = r[:, i * LANES:(i + 1) * LANES]
            sim[v, pl.ds(i, tc, stride=step_rows), :] = r[:, (SUBLANES + i) * LANES:(SUBLANES + i + 1) * LANES]

    a_re = [are_ref[v * SUBLANES:(v + 1) * SUBLANES, :] for v in range(SUPER)]
    a_im = [aim_ref[v * SUBLANES:(v + 1) * SUBLANES, :] for v in range(SUPER)]
    init = tuple(hre_ref[0, v * SUBLANES:(v + 1) * SUBLANES, :] for v in range(SUPER)) + \
        tuple(him_ref[0, v * SUBLANES:(v + 1) * SUBLANES, :] for v in range(SUPER))

    def step(t, carry):
        row = pl.multiple_of(t * step_rows, step_rows)
        new_re, new_im = [], []
        for v in range(SUPER):
            hr, hi = carry[v], carry[SUPER + v]
            nr = a_re[v] * hr - a_im[v] * hi + sre[v, pl.ds(row, step_rows), :]
            ni = a_re[v] * hi + a_im[v] * hr + sim[v, pl.ds(row, step_rows), :]
            sre[v, pl.ds(row, step_rows), :] = nr
            sim[v, pl.ds(row, step_rows), :] = ni
            new_re.append(nr)
            new_im.append(ni)
        return tuple(new_re) + tuple(new_im)

    fin = lax.fori_loop(0, tc, step, init, unroll=8)
    for v in range(SUPER):
        hre_ref[0, v * SUBLANES:(v + 1) * SUBLANES, :] = fin[v]
        him_ref[0, v * SUBLANES:(v + 1) * SUBLANES, :] = fin[SUPER + v]

    for v in range(SUPER):
        cols = [sre[v, pl.ds(i, tc, stride=step_rows), :] for i in range(SUBLANES)]
        cols += [sim[v, pl.ds(i, tc, stride=step_rows), :] for i in range(SUBLANES)]
        hcat = jnp.concatenate(cols, axis=1).astype(BF16)
        y = jnp.dot(hcat, wc_ref[v], preferred_element_type=F32)
        y_ref[:, v * sgw:(v + 1) * sgw] = y + d_ref[:, v * sgw:(v + 1) * sgw] * u[:, v * sgw:(v + 1) * sgw]


def _s5_seq(u, wb, wc, a_re, a_im, d, h0re, h0im, l, nb, t, tc):
    s5w = u.shape[1]
    sgw = s5w // SUPER
    nk = t // tc
    rows = a_re.shape[1]
    st_spec = pl.BlockSpec((1, rows, LANES), lambda b, k: (b, 0, 0))
    return pl.pallas_call(
        functools.partial(_s5_seq_kernel, tc=tc, sgw=sgw),
        grid=(nb, nk),
        in_specs=[pl.BlockSpec((tc, s5w), lambda b, k: (b * nk + k, 0)),
                  _resident(wb, l), _resident(wc, l), _resident(a_re, l), _resident(a_im, l),
                  _resident(d, l), st_spec, st_spec],
        out_specs=[pl.BlockSpec((tc, s5w), lambda b, k: (b * nk + k, 0)), st_spec, st_spec],
        out_shape=[jax.ShapeDtypeStruct((nb * t, s5w), F32),
                   jax.ShapeDtypeStruct((nb, rows, LANES), F32), jax.ShapeDtypeStruct((nb, rows, LANES), F32)],
        scratch_shapes=[pltpu.VMEM((SUPER, tc * SUBLANES, LANES), F32),
                        pltpu.VMEM((SUPER, tc * SUBLANES, LANES), F32)],
        compiler_params=_params(("arbitrary", "arbitrary")),
        name="s5_seq",
    )(u, wb, wc, a_re, a_im, d, h0re, h0im)


def _s5_planes_kernel(u_ref, wb_ref, wc_ref, are_ref, aim_ref, d_ref, h0re_ref, h0im_ref,
                      y_ref, hre_ref, him_ref, *, nt, nseq, half):
    u = u_ref[...]
    r = jnp.dot(u.astype(BF16), wb_ref[0], preferred_element_type=F32)
    a_re, a_im = are_ref[...], aim_ref[...]
    hr, hi = h0re_ref[...], h0im_ref[...]
    hs_re, hs_im = [], []
    for t in range(nt):
        br = r[t * nseq:(t + 1) * nseq, :half]
        bi = r[t * nseq:(t + 1) * nseq, half:]
        hr, hi = a_re * hr - a_im * hi + br, a_re * hi + a_im * hr + bi
        hs_re.append(hr)
        hs_im.append(hi)
    hre_ref[...] = hr
    him_ref[...] = hi
    hcat = jnp.concatenate([jnp.concatenate(hs_re, axis=0), jnp.concatenate(hs_im, axis=0)], axis=1)
    y = jnp.dot(hcat.astype(BF16), wc_ref[0], preferred_element_type=F32)
    y_ref[...] = y + d_ref[...] * u


def _s5_planes(u, u_blk, wb, wc, a_re, a_im, d, h0re, h0im, l, nt, nseq):
    n, s5w = nt * nseq, u.shape[1]
    sgw = s5w // SUPER
    nstate = h0re.shape[2]
    half = nstate // SUPER
    return pl.pallas_call(
        functools.partial(_s5_planes_kernel, nt=nt, nseq=nseq, half=half),
        grid=(SUPER,),
        in_specs=[pl.BlockSpec((n, sgw), lambda v: (u_blk, v)),
                  pl.BlockSpec((None, 1) + wb.shape[2:], lambda v: (l, v, 0, 0)),
                  pl.BlockSpec((None, 1) + wc.shape[2:], lambda v: (l, v, 0, 0)),
                  pl.BlockSpec((None, 1, half), lambda v: (l, 0, v)),
                  pl.BlockSpec((None, 1, half), lambda v: (l, 0, v)),
                  pl.BlockSpec((None, 1, sgw), lambda v: (l, 0, v)),
                  pl.BlockSpec((None, nseq, half), lambda v: (l, 0, v)),
                  pl.BlockSpec((None, nseq, half), lambda v: (l, 0, v))],
        out_specs=[pl.BlockSpec((n, sgw), lambda v: (0, v)),
                   pl.BlockSpec((nseq, half), lambda v: (0, v)), pl.BlockSpec((nseq, half), lambda v: (0, v))],
        out_shape=[jax.ShapeDtypeStruct((n, s5w), F32),
                   jax.ShapeDtypeStruct((nseq, nstate), F32), jax.ShapeDtypeStruct((nseq, nstate), F32)],
        compiler_params=_params(("arbitrary",)),
        name="s5_planes",
    )(u, wb, wc, a_re, a_im, d, h0re, h0im)


CONV_ROW_TILE = 32
CONV_COL_TILE = 512


def _conv_seq_kernel(c_ref, halo_ref, w_ref, b_ref, g_ref, bt_ref, out_ref, xp_ref, sh_ref, y_ref,
                     *, tm, kk, halo, blocks_per_seq):
    i = pl.program_id(0)
    start = (i % blocks_per_seq) == 0
    ncol = c_ref.shape[1]
    xp_ref[0:halo, 0:ncol] = jnp.where(start, 0.0, halo_ref[...])
    xp_ref[halo:halo + tm, 0:ncol] = c_ref[...]
    ncopy = tm + halo - SUBLANES
    for b in range(1, SUBLANES):
        sh_ref[b - 1, :, 0:ncol] = xp_ref[pl.ds(b, ncopy), 0:ncol]
    rt = CONV_ROW_TILE
    ngrp = rt // SUBLANES
    for lc in range(ncol // CONV_COL_TILE):
        cols = slice(lc * CONV_COL_TILE, (lc + 1) * CONV_COL_TILE)

        def tile(r, carry, cols=cols):
            r0 = pl.multiple_of(r * rt, rt)
            acc = [jnp.broadcast_to(b_ref[:, cols], (SUBLANES, CONV_COL_TILE))] * ngrp
            for j in range(kk):
                a, b = divmod(halo - j, SUBLANES)
                k = kk - 1 - j
                w8 = w_ref[SUBLANES * k:SUBLANES * (k + 1), cols]
                for q in range(ngrp):
                    rows = pl.ds(r0 + SUBLANES * (a + q), SUBLANES)
                    xs = xp_ref[rows, cols] if b == 0 else sh_ref[b - 1, rows, cols]
                    acc[q] = acc[q] + w8 * xs
            for q in range(ngrp):
                y_ref[pl.ds(r0 + SUBLANES * q, SUBLANES), cols] = acc[q]
            return carry

        lax.fori_loop(0, tm // rt, tile, 0)
    out_ref[...] = jax.nn.silu(_ln(y_ref[...], g_ref[...], bt_ref[...]))


def _conv_seq(c, n, w, b, g, bt, l, t, tm, kk):
    cw = c.shape[1]
    halo = 32
    assert kk - 1 <= halo and tm % halo == 0 and t % tm == 0
    hb = tm // halo
    return pl.pallas_call(
        functools.partial(_conv_seq_kernel, tm=tm, kk=kk, halo=halo, blocks_per_seq=t // tm),
        grid=(n // tm,),
        in_specs=[pl.BlockSpec((tm, cw), lambda i: (i, 0)),
                  pl.BlockSpec((halo, cw), lambda i: (jnp.maximum(i * hb - 1, 0), 0)),
                  _resident(w, l), _resident(b, l), _resident(g, l), _resident(bt, l)],
        out_specs=pl.BlockSpec((tm, cw), lambda i: (i, 0)),
        out_shape=jax.ShapeDtypeStruct((n, cw), F32),
        scratch_shapes=[pltpu.VMEM((tm + halo, cw + LANES), F32),
                        pltpu.VMEM((SUBLANES - 1, tm + halo - SUBLANES, cw + LANES), F32),
                        pltpu.VMEM((tm, cw), F32)],
        compiler_params=_params(("arbitrary",)),
        name="conv_seq",
    )(c, c, w, b, g, bt)


def _conv_planes_kernel(cache_ref, c_ref, w_ref, b_ref, g_ref, bt_ref, out_ref, *, nt, kk):
    rows, cw = c_ref.shape[1], c_ref.shape[2]
    gam, bet = g_ref[...], bt_ref[...]
    for t in range(nt):
        acc = jnp.broadcast_to(b_ref[...], (rows, cw))
        for k in range(kk):
            p = t + k
            xs = cache_ref[p] if p < kk - 1 else c_ref[p - (kk - 1)]
            w8 = w_ref[SUBLANES * k:SUBLANES * (k + 1), :]
            acc = acc + jnp.concatenate([w8] * (rows // SUBLANES), axis=0) * xs
        out_ref[t] = jax.nn.silu(_ln(acc, gam, bet))


def _conv_planes(cache_t, c_t, w8, b, g, bt, l, kk, rows):
    nt, nseq, cw = c_t.shape
    return pl.pallas_call(
        functools.partial(_conv_planes_kernel, nt=nt, kk=kk),
        grid=(nseq // rows,),
        in_specs=[pl.BlockSpec((None, kk - 1, rows, cw), lambda i: (l, 0, i, 0)),
                  pl.BlockSpec((nt, rows, cw), lambda i: (0, i, 0)),
                  _resident(w8, l), _resident(b, l), _resident(g, l), _resident(bt, l)],
        out_specs=pl.BlockSpec((nt, rows, cw), lambda i: (0, i, 0)),
        out_shape=jax.ShapeDtypeStruct((nt, nseq, cw), F32),
        compiler_params=_params(("arbitrary",)),
        name="conv_planes",
    )(cache_t, c_t, w8, b, g, bt)


def _mix_out_kernel(y5a_ref, y5b_ref, caa_ref, cab_ref, xa_ref, xb_ref, wglu_ref, wout_ref, g_ref, b_ref, out_ref,
                    *, alpha, s5w, n_a):
    first = pl.program_id(0) < n_a
    rs = xa_ref.shape[0] // ROW_SPLIT
    for s in range(ROW_SPLIT):
        rows = slice(s * rs, (s + 1) * rs)
        g5 = jax.nn.gelu(_pick_rows(first, y5a_ref, y5b_ref, rows))
        gate = jnp.dot(g5.astype(BF16), wglu_ref[...], preferred_element_type=F32)
        s5o = g5 * jax.nn.sigmoid(gate)
        mix = jnp.dot(s5o.astype(BF16), wout_ref[:s5w, :], preferred_element_type=F32)
        ca = _pick_rows(first, caa_ref, cab_ref, rows)
        mix = mix + jnp.dot(ca.astype(BF16), wout_ref[s5w:, :], preferred_element_type=F32)
        x = _pick_rows(first, xa_ref, xb_ref, rows)
        out_ref[rows, :] = _ln(alpha * x + mix, g_ref[...], b_ref[...])


def _mix_out(y5a, y5b, caa, cab, xa, n_a, xb, b_blk, wglu, wout, g, b, l, alpha, tm):
    d = xa.shape[1]
    s5w, cw = y5a.shape[1], caa.shape[1]
    n = (n_a + 1) * tm
    return pl.pallas_call(
        functools.partial(_mix_out_kernel, alpha=alpha, s5w=s5w, n_a=n_a),
        grid=(n_a + 1,),
        in_specs=_two_sources(n_a, 0, tm, s5w) + _two_sources(n_a, 0, tm, cw) + _two_sources(n_a, b_blk, tm, d) +
        [_resident(wglu, l), _resident(wout, l), _resident(g, l), _resident(b, l)],
        out_specs=pl.BlockSpec((tm, d), lambda i: (i, 0)),
        out_shape=jax.ShapeDtypeStruct((n, d), F32),
        compiler_params=_params(("arbitrary",)),
        name="mix_out",
    )(y5a, y5b, caa, cab, xa, xb, wglu, wout, g, b)


FFN_HALO = 16


def _ffn_kernel(*refs, alpha, tm, planes, nseq, blocks_per_seq):
    if planes:
        (x_ref, hg_c_ref, hv_c_ref, wg_ref, wv_ref, wd_ref, cwg_ref, cwv_ref, cbg_ref, cbv_ref, g_ref, b_ref,
         out_ref, tg_ref, tv_ref, xb_ref) = refs
    else:
        (x_hbm, halo_ref, wg_ref, wv_ref, wd_ref, cwg_ref, cwv_ref, cbg_ref, cbv_ref, g_ref, b_ref,
         out_ref, tg_ref, tv_ref, xb_ref, x_ref, x_sem) = refs
    i = pl.program_id(0)
    j = pl.program_id(1)

    def x_copy(blk):
        return pltpu.make_async_copy(x_hbm.at[pl.ds(pl.multiple_of(blk * tm, tm), tm), :], x_ref, x_sem)

    @pl.when(j == 0)
    def _():
        if planes:
            xb_ref[...] = x_ref[...].astype(BF16)
            out_ref[...] = jnp.zeros_like(out_ref)
        else:
            @pl.when(i == 0)
            def _():
                x_copy(0).start()

            x_copy(i).wait()
            start = (i % blocks_per_seq) == 0
            xb_ref[0:FFN_HALO, :] = jnp.where(start, 0.0, halo_ref[...]).astype(BF16)
            xb_ref[FFN_HALO:, :] = x_ref[...].astype(BF16)
            out_ref[...] = alpha * x_ref[...]

    if not planes:
        @pl.when(jnp.logical_and(j == 1, i + 1 < pl.num_programs(0)))
        def _():
            x_copy(i + 1).start()

    xb = xb_ref[...]
    hg = jnp.dot(xb, wg_ref[...], preferred_element_type=F32)
    hv = jnp.dot(xb, wv_ref[...], preferred_element_type=F32)

    def conv3(h, hist_ref, cw_ref, cb_ref, tail_ref):
        w0, w1, w2 = cw_ref[0:1, :], cw_ref[1:2, :], cw_ref[2:3, :]
        if planes:
            hp = jnp.concatenate([hist_ref[...], h], axis=0)
            cur, prev1, prev2 = hp[2 * nseq:], hp[nseq:nseq + tm], hp[:tm]
            tail_ref[...] = h[tm - 2 * nseq:, :]
        else:
            cur = h[FFN_HALO:, :]
            prev1 = h[FFN_HALO - 1:FFN_HALO - 1 + tm, :]
            prev2 = h[FFN_HALO - 2:FFN_HALO - 2 + tm, :]
            tail_ref[...] = h[FFN_HALO + tm - SUBLANES:, :]
        return w2 * cur + w1 * prev1 + w0 * prev2 + cb_ref[...]

    cg = conv3(hg, hg_c_ref if planes else None, cwg_ref, cbg_ref, tg_ref)
    cv = conv3(hv, hv_c_ref if planes else None, cwv_ref, cbv_ref, tv_ref)
    act = (jax.nn.silu(cg) * cv).astype(BF16)
    out_ref[...] += jnp.dot(act, wd_ref[...], preferred_element_type=F32)

    @pl.when(j == pl.num_programs(1) - 1)
    def _():
        r = alpha * x_ref[...] + out_ref[...] if planes else out_ref[...]
        out_ref[...] = _ln(r, g_ref[...], b_ref[...])


def _ffn(x, n, w_up, w_down, conv_w, conv_b, g, b, l, alpha, tm, tf, *, cache=None, x_blk=0, t=None, nseq=None):
    d = x.shape[1]
    dff = w_down.shape[1]
    nj = dff // tf
    fk = conv_w.shape[1]
    planes = cache is not None
    wspecs = [pl.BlockSpec((None, d, tf), lambda i, j: (l, 0, j)),
              pl.BlockSpec((None, d, tf), lambda i, j: (l, 0, nj + j)),
              pl.BlockSpec((None, tf, d), lambda i, j: (l, j, 0)),
              pl.BlockSpec((None, fk, tf), lambda i, j: (l, 0, j)),
              pl.BlockSpec((None, fk, tf), lambda i, j: (l, 0, nj + j)),
              pl.BlockSpec((None, 1, tf), lambda i, j: (l, 0, j)),
              pl.BlockSpec((None, 1, tf), lambda i, j: (l, 0, nj + j)),
              pl.BlockSpec((None, 1, d), lambda i, j: (l, 0, 0)), pl.BlockSpec((None, 1, d), lambda i, j: (l, 0, 0))]
    wargs = [w_up, w_up, w_down, conv_w, conv_w, conv_b, conv_b, g, b]
    if planes:
        assert n == tm
        hist = cache.shape[1]
        in_specs = [pl.BlockSpec((tm, d), lambda i, j: (x_blk, 0)),
                    pl.BlockSpec((None, hist, tf), lambda i, j: (l, 0, j)),
                    pl.BlockSpec((None, hist, tf), lambda i, j: (l, 0, nj + j))] + wspecs
        args = [x, cache, cache] + wargs
        tail_rows, xb_rows, bps = hist, tm, 1
        scratch = [pltpu.VMEM((xb_rows, d), BF16)]
    else:
        hb = tm // FFN_HALO
        in_specs = [pl.BlockSpec(memory_space=pl.ANY),
                    pl.BlockSpec((FFN_HALO, d), lambda i, j: (jnp.maximum(i * hb - 1, 0), 0))] + wspecs
        args = [x, x] + wargs
        tail_rows, xb_rows, bps = SUBLANES, tm + FFN_HALO, t // tm
        scratch = [pltpu.VMEM((xb_rows, d), BF16), pltpu.VMEM((tm, d), F32), pltpu.SemaphoreType.DMA(())]
    nblk = n // tm
    tail_spec = pl.BlockSpec((tail_rows, tf), lambda i, j: (i, j))
    return pl.pallas_call(
        functools.partial(_ffn_kernel, alpha=alpha, tm=tm, planes=planes, nseq=nseq, blocks_per_seq=bps),
        grid=(nblk, nj),
        in_specs=in_specs,
        out_specs=[pl.BlockSpec((tm, d), lambda i, j: (i, 0)), tail_spec, tail_spec],
        out_shape=[jax.ShapeDtypeStruct((n, d), F32),
                   jax.ShapeDtypeStruct((nblk * tail_rows, dff), F32),
                   jax.ShapeDtypeStruct((nblk * tail_rows, dff), F32)],
        scratch_shapes=scratch,
        compiler_params=_params(("arbitrary", "arbitrary")),
        name="ffn_planes" if planes else "ffn_seq",
    )(*args)


def _pe_kernel(x_ref, p_ref, wg_ref, wp_ref, g_ref, b_ref, out_ref, *, alpha):
    rs = x_ref.shape[0] // ROW_SPLIT
    for s in range(ROW_SPLIT):
        rows = slice(s * rs, (s + 1) * rs)
        x = x_ref[rows, :]
        gate = jax.nn.sigmoid(jnp.dot(x.astype(BF16), wg_ref[...], preferred_element_type=F32))
        e = jnp.dot(p_ref[rows, :].astype(BF16), wp_ref[...], preferred_element_type=F32)
        out_ref[rows, :] = _ln(alpha * x + gate * e, g_ref[...], b_ref[...])


def _pe(x, p, wg, wp, g, b, l, alpha, tm):
    n, d = x.shape
    pd = p.shape[2]
    return pl.pallas_call(
        functools.partial(_pe_kernel, alpha=alpha),
        grid=(n // tm,),
        in_specs=[pl.BlockSpec((tm, d), lambda i: (i, 0)), pl.BlockSpec((None, tm, pd), lambda i: (l, i, 0)),
                  _resident(wg, l), _resident(wp, l), _resident(g, l), _resident(b, l)],
        out_specs=pl.BlockSpec((tm, d), lambda i: (i, 0)),
        out_shape=jax.ShapeDtypeStruct((n, d), F32),
        compiler_params=_params(("arbitrary",)),
        name="pe_embed",
    )(x, p, wg, wp, g, b)


def _pe2_kernel(xa_ref, xb_ref, pa_ref, pb_ref, wg_ref, wp_ref, g_ref, b_ref, out_ref, *, alpha, n_a):
    first = pl.program_id(0) < n_a
    rs = xa_ref.shape[0] // ROW_SPLIT
    for s in range(ROW_SPLIT):
        rows = slice(s * rs, (s + 1) * rs)
        x = _pick_rows(first, xa_ref, xb_ref, rows)
        gate = jax.nn.sigmoid(jnp.dot(x.astype(BF16), wg_ref[...], preferred_element_type=F32))
        e = jnp.dot(_pick_rows(first, pa_ref, pb_ref, rows).astype(BF16), wp_ref[...], preferred_element_type=F32)
        out_ref[rows, :] = _ln(alpha * x + gate * e, g_ref[...], b_ref[...])


def _pe2(xa, xb, pa, pb, wg, wp, g, b, l, alpha, tm):
    d = xa.shape[1]
    pd = pa.shape[2]
    n_a = xa.shape[0] // tm
    n = (n_a + 1) * tm
    return pl.pallas_call(
        functools.partial(_pe2_kernel, alpha=alpha, n_a=n_a),
        grid=(n_a + 1,),
        in_specs=_two_sources(n_a, 0, tm, d) +
        [pl.BlockSpec((None, tm, pd), lambda i: (l, jnp.minimum(i, n_a - 1), 0)),
         pl.BlockSpec((None, tm, pd), lambda i: (l, 0, 0)),
         _resident(wg, l), _resident(wp, l), _resident(g, l), _resident(b, l)],
        out_specs=pl.BlockSpec((tm, d), lambda i: (i, 0)),
        out_shape=jax.ShapeDtypeStruct((n, d), F32),
        compiler_params=_params(("arbitrary",)),
        name="pe_embed2",
    )(xa, xb, pa, pb, wg, wp, g, b)


def _s5_discretise(lam_re, lam_im, log_dt, b_re, b_im):
    lr, li = lam_re.astype(F32), lam_im.astype(F32)
    dt = jnp.exp(log_dt.astype(F32))[..., None]
    mag = jnp.exp(lr * dt)
    ab_re = mag * jnp.cos(li * dt)
    ab_im = mag * jnp.sin(li * dt)
    num_re, num_im = ab_re - 1.0, ab_im
    den = lr * lr + li * li
    q_re = (num_re * lr + num_im * li) / den
    q_im = (num_im * lr - num_re * li) / den
    br, bi = b_re.astype(F32), b_im.astype(F32)
    bb_re = q_re[..., None] * br - q_im[..., None] * bi
    bb_im = q_re[..., None] * bi + q_im[..., None] * br
    return ab_re, ab_im, bb_re, bb_im


def _blockdiag_kernel(tbr_ref, tbi_ref, tcr_ref, tci_ref, eb_ref, ec_ref, wb_ref, wc_ref, *, gs, h, p):
    def place(t_ref, e_ref, rows_per_block, cols_per_block):
        tiled = jnp.dot(t_ref[...].astype(BF16), e_ref[...], preferred_element_type=F32)
        rg = lax.broadcasted_iota(jnp.int32, tiled.shape, 0) // rows_per_block
        cg = lax.broadcasted_iota(jnp.int32, tiled.shape, 1) // cols_per_block
        return jnp.where(rg == cg, tiled, 0.0).astype(BF16)

    wb_ref[:, :gs * p] = place(tbr_ref, eb_ref, h, p)
    wb_ref[:, gs * p:] = place(tbi_ref, eb_ref, h, p)
    wc_ref[:gs * p, :] = place(tcr_ref, ec_ref, p, h)
    wc_ref[gs * p:, :] = place(tci_ref, ec_ref, p, h)


def _s5_block_weights(bb_re, bb_im, c_re, c_im):
    nl, g, p, h = bb_re.shape
    gs = g // SUPER
    n = nl * SUPER
    tb = [v.transpose(0, 1, 3, 2).reshape(n, gs * h, p) for v in (bb_re, bb_im)]
    tc = [v.astype(F32).transpose(0, 1, 3, 2).reshape(n, gs * p, h) for v in (c_re, -c_im)]
    eb = jnp.tile(jnp.eye(p, dtype=BF16), (1, gs))
    ec = jnp.tile(jnp.eye(h, dtype=BF16), (1, gs))
    blk = lambda a: pl.BlockSpec((None,) + a.shape[1:], lambda i: (i, 0, 0))
    whole = lambda a: pl.BlockSpec(a.shape, lambda i: (0, 0))
    wb, wc = pl.pallas_call(
        functools.partial(_blockdiag_kernel, gs=gs, h=h, p=p),
        grid=(n,),
        in_specs=[blk(tb[0]), blk(tb[1]), blk(tc[0]), blk(tc[1]), whole(eb), whole(ec)],
        out_specs=[pl.BlockSpec((None, gs * h, 2 * gs * p), lambda i: (i, 0, 0)),
                   pl.BlockSpec((None, 2 * gs * p, gs * h), lambda i: (i, 0, 0))],
        out_shape=[jax.ShapeDtypeStruct((n, gs * h, 2 * gs * p), BF16),
                   jax.ShapeDtypeStruct((n, 2 * gs * p, gs * h), BF16)],
        compiler_params=_params(("arbitrary",)),
        name="s5_blockdiag",
    )(tb[0], tb[1], tc[0], tc[1], eb, ec)
    return wb.reshape(nl, SUPER, gs * h, 2 * gs * p), wc.reshape(nl, SUPER, 2 * gs * p, gs * h)


def _rows(v):
    return v.astype(F32).reshape(v.shape[0], 1, -1)


def kernel(x_prompt, x_sample, state_s5_re, state_s5_im, cache_conv, cache_ffn_conv, p_prompt, p_sample,
           w_in, s5_lam_re, s5_lam_im, s5_log_dt, s5_b_re, s5_b_im, s5_c_re, s5_c_im, s5_d, s5_w_glu,
           conv_w, conv_b, conv_ln_g, conv_ln_b, w_out, ln1_g, ln1_b,
           ffn_w_up, ffn_conv_w, ffn_conv_b, ffn_w_down, ln2_g, ln2_b,
           pe_w, pe_w_gate, ln3_g, ln3_b):
    nb, t, d = x_prompt.shape
    ns, ts, _ = x_sample.shape
    depth = w_in.shape[0]
    groups, nstate_g = s5_lam_re.shape[1], s5_lam_re.shape[2]
    s5w = groups * s5_d.shape[2]
    cw = d - s5w
    nstate = groups * nstate_g
    kk = conv_w.shape[1]
    fk = ffn_conv_w.shape[1]
    dff = ffn_w_down.shape[1]
    assert fk == 3 and ts >= fk - 1 and ts <= kk - 1 and t >= kk - 1
    alpha = (2.0 * depth) ** 0.25
    tm, tm_ffn, tf, tc, tconv = 512, 1024, 512, 512, 512

    xp = x_prompt.reshape(nb * t, d)
    xs = x_sample.transpose(1, 0, 2).reshape(ts * ns, d)
    zero_state = jnp.zeros((nb, nstate // LANES, LANES), F32)

    w_in_bf, wglu_bf, wout_bf = w_in.astype(BF16), s5_w_glu.astype(BF16), w_out.astype(BF16)
    wup_bf, wdown_bf = ffn_w_up.astype(BF16), ffn_w_down.astype(BF16)
    wpe_bf, wgate_bf = pe_w.astype(BF16), pe_w_gate.astype(BF16)
    ab_re, ab_im, bb_re, bb_im = _s5_discretise(s5_lam_re, s5_lam_im, s5_log_dt, s5_b_re, s5_b_im)
    wb, wc = _s5_block_weights(bb_re, bb_im, s5_c_re, s5_c_im)
    a_tile_re, a_tile_im = (v.reshape(depth, nstate // LANES, LANES) for v in (ab_re, ab_im))
    a_row_re, a_row_im = (v.reshape(depth, 1, nstate) for v in (ab_re, ab_im))
    d_rows = _rows(s5_d.reshape(depth, s5w))
    conv_w1 = conv_w.astype(F32)
    conv_w8 = jnp.repeat(conv_w1, SUBLANES, axis=1)
    conv_b_rows, cg_rows, cb_rows = _rows(conv_b), _rows(conv_ln_g), _rows(conv_ln_b)
    fcw, fcb = ffn_conv_w.astype(F32), _rows(ffn_conv_b)
    l1g, l1b, l2g, l2b, l3g, l3b = (_rows(v) for v in (ln1_g, ln1_b, ln2_g, ln2_b, ln3_g, ln3_b))
    pp = p_prompt.reshape(depth, nb * t, -1)
    ps = p_sample.transpose(0, 2, 1, 3).reshape(depth, ts * ns, -1)
    h0_re = state_s5_re.reshape(depth, ns, nstate).astype(F32)
    h0_im = state_s5_im.reshape(depth, ns, nstate).astype(F32)
    cache_t = cache_conv.transpose(0, 2, 1, 3)
    fcache = cache_ffn_conv.transpose(0, 2, 1, 3).reshape(depth, (fk - 1) * ns, 2 * dff)

    p_re, p_im, p_cv, p_tg, p_tv = [], [], [], [], []
    s_re, s_im, s_c, s_tg, s_tv = [], [], [], [], []
    n_p, n_s = nb * t, ts * ns
    assert n_s == tm and n_p % tm == 0
    nblk_p = n_p // tm
    x_seq, x_pl, x_pl_blk = xp, xs, 0
    for l in range(depth):
        u, c = _in_proj(x_seq, nblk_p, x_pl, x_pl_blk, w_in_bf, l, s5w, cw, tm)
        c_pl = c[n_p:]
        y5p, hre_p, him_p = _s5_seq(u, wb, wc, a_tile_re, a_tile_im, d_rows, zero_state, zero_state, l, nb, t, tc)
        y5s, hre_s, him_s = _s5_planes(u, nblk_p, wb, wc, a_row_re, a_row_im, d_rows, h0_re, h0_im, l, ts, ns)
        cap = _conv_seq(c, n_p, conv_w8, conv_b_rows, cg_rows, cb_rows, l, t, tconv, kk)
        cas = _conv_planes(cache_t, c_pl.reshape(ts, ns, cw), conv_w8, conv_b_rows, cg_rows, cb_rows, l, kk, 32)
        x1 = _mix_out(y5p, y5s, cap, cas.reshape(n_s, cw), x_seq, nblk_p, x_pl, x_pl_blk,
                      wglu_bf, wout_bf, l1g, l1b, l, alpha, tm)
        x2p, tg_p, tv_p = _ffn(x1, n_p, wup_bf, wdown_bf, fcw, fcb, l2g, l2b, l, alpha, tm_ffn, tf, t=t)
        x2s, tg_s, tv_s = _ffn(x1, n_s, wup_bf, wdown_bf, fcw, fcb, l2g, l2b, l, alpha, n_s, tf,
                               cache=fcache, x_blk=nblk_p, nseq=ns)
        if l + 1 < depth:
            x_all = _pe2(x2p, x2s, pp, ps, wgate_bf, wpe_bf, l3g, l3b, l, alpha, tm)
            x_seq, x_pl, x_pl_blk = x_all, x_all, nblk_p
        else:
            xp = _pe(x2p, pp, wgate_bf, wpe_bf, l3g, l3b, l, alpha, tm)
            xs = _pe(x2s, ps, wgate_bf, wpe_bf, l3g, l3b, l, alpha, tm)
        p_re.append(hre_p)
        p_im.append(him_p)
        p_cv.append(jnp.stack([c[(b + 1) * t - (kk - 1):(b + 1) * t] for b in range(nb)]))
        p_tg.append(tg_p)
        p_tv.append(tv_p)
        s_re.append(hre_s)
        s_im.append(him_s)
        s_c.append(c_pl)
        s_tg.append(tg_s)
        s_tv.append(tv_s)

    y_prompt = xp.reshape(nb, t, d)
    y_sample = xs.reshape(ts, ns, d).transpose(1, 0, 2)
    bps = t // tm_ffn
    p_tails = jnp.concatenate([jnp.stack(p_tg), jnp.stack(p_tv)], axis=2).reshape(depth, nb, bps, SUBLANES, 2 * dff)
    p_ff = p_tails[:, :, bps - 1, SUBLANES - (fk - 1):]
    s_cv = jnp.concatenate([cache_conv[:, :, ts:], jnp.stack(s_c).reshape(depth, ts, ns, cw).transpose(0, 2, 1, 3)],
                           axis=2)
    s_ff = jnp.concatenate([jnp.stack(s_tg), jnp.stack(s_tv)], axis=2)
    s_ff = s_ff.reshape(depth, fk - 1, ns, 2 * dff).transpose(0, 2, 1, 3)
    return (y_prompt, y_sample,
            jnp.stack(p_re).reshape(depth, nb, groups, nstate_g), jnp.stack(p_im).reshape(depth, nb, groups, nstate_g),
            jnp.stack(p_cv), p_ff,
            jnp.stack(s_re).reshape(depth, ns, groups, nstate_g), jnp.stack(s_im).reshape(depth, ns, groups, nstate_g),
            s_cv, s_ff)
```

```python
import functools

import jax
import jax.numpy as jnp
from jax import lax
from jax.experimental import pallas as pl
from jax.experimental.pallas import tpu as pltpu

F32 = jnp.float32
BF16 = jnp.bfloat16

LN_EPS = 1e-5
V7X_VMEM_LIMIT_BYTES = 56 * 1024 * 1024
SUBLANES = 8
LANES = 128
SUPER = 4
ROW_SPLIT = 2


def _params(sem):
    return pltpu.CompilerParams(dimension_semantics=sem, vmem_limit_bytes=V7X_VMEM_LIMIT_BYTES)


def _resident(stacked, l):
    shape = stacked.shape[1:]
    nd = len(shape)
    return pl.BlockSpec((None,) + shape, lambda *_: (l,) + (0,) * nd, pipeline_mode=pl.Buffered(1))


def _ln(r, g, b):
    mu = jnp.mean(r, axis=-1, keepdims=True)
    d = r - mu
    var = jnp.mean(d * d, axis=-1, keepdims=True)
    return d * lax.rsqrt(var + LN_EPS) * g + b


def _pick_rows(first_source, a_ref, b_ref, rows):
    return jnp.where(first_source, a_ref[rows, :], b_ref[rows, :])


def _in_proj_kernel(xa_ref, xb_ref, w_ref, u_ref, c_ref, *, s5w, cw, n_a):
    first = pl.program_id(0) < n_a
    rs = xa_ref.shape[0] // ROW_SPLIT
    for s in range(ROW_SPLIT):
        rows = slice(s * rs, (s + 1) * rs)
        xb = _pick_rows(first, xa_ref, xb_ref, rows).astype(BF16)
        u_ref[rows, :] = jnp.dot(xb, w_ref[:, :s5w], preferred_element_type=F32)
        cv = jnp.dot(xb, w_ref[:, s5w:s5w + cw], preferred_element_type=F32)
        cg = jnp.dot(xb, w_ref[:, s5w + cw:], preferred_element_type=F32)
        c_ref[rows, :] = cv * jax.nn.sigmoid(cg)


def _two_sources(n_a, b_blk, tm, width):
    return [pl.BlockSpec((tm, width), lambda i: (jnp.minimum(i, n_a - 1), 0)),
            pl.BlockSpec((tm, width), lambda i: (b_blk, 0))]


def _in_proj(xa, n_a, xb, b_blk, w_in_bf, l, s5w, cw, tm):
    d = xa.shape[1]
    n = (n_a + 1) * tm
    return pl.pallas_call(
        functools.partial(_in_proj_kernel, s5w=s5w, cw=cw, n_a=n_a),
        grid=(n_a + 1,),
        in_specs=_two_sources(n_a, b_blk, tm, d) + [_resident(w_in_bf, l)],
        out_specs=[pl.BlockSpec((tm, s5w), lambda i: (i, 0)), pl.BlockSpec((tm, cw), lambda i: (i, 0))],
        out_shape=[jax.ShapeDtypeStruct((n, s5w), F32), jax.ShapeDtypeStruct((n, cw), F32)],
        compiler_params=_params(("arbitrary",)),
        name="in_proj",
    )(xa, xb, w_in_bf)


def _s5_seq_kernel(u_ref, wb_ref, wc_ref, are_ref, aim_ref, d_ref, h0re_ref, h0im_ref,
                   y_ref, hre_ref, him_ref, sre, sim, *, tc, sgw):
    step_rows = SUBLANES
    k = pl.program_id(1)

    @pl.when(k == 0)
    def _():
        hre_ref[...] = h0re_ref[...]
        him_ref[...] = h0im_ref[...]

    u = u_ref[...]
    ub = u.astype(BF16)
    for v in range(SUPER):
        r = jnp.dot(ub[:, v * sgw:(v + 1) * sgw], wb_ref[v], preferred_element_type=F32)
        for i in range(SUBLANES):
            sre[v, pl.ds(i, tc, stride=step_rows), :] = r[:, i * LANES:(i + 1) * LANES]
            sim[v, pl.ds(i, tc, stride=step_rows), :] = r[:, (SUBLANES + i) * LANES:(SUBLANES + i + 1) * LANES]

    a_re = [are_ref[v * SUBLANES:(v + 1) * SUBLANES, :] for v in range(SUPER)]
    a_im = [aim_ref[v * SUBLANES:(v + 1) * SUBLANES, :] for v in range(SUPER)]
    init = tuple(hre_ref[0, v * SUBLANES:(v + 1) * SUBLANES, :] for v in range(SUPER)) + \
        tuple(him_ref[0, v * SUBLANES:(v + 1) * SUBLANES, :] for v in range(SUPER))

    def step(t, carry):
        row = pl.multiple_of(t * step_rows, step_rows)
        new_re, new_im = [], []
        for v in range(SUPER):
            hr, hi = carry[v], carry[SUPER + v]
            nr = a_re[v] * hr - a_im[v] * hi + sre[v, pl.ds(row, step_rows), :]
            ni = a_re[v] * hi + a_im[v] * hr + sim[v, pl.ds(row, step_rows), :]
            sre[v, pl.ds(row, step_rows), :] = nr
            sim[v, pl.ds(row, step_rows), :] = ni
            new_re.append(nr)
            new_im.append(ni)
        return tuple(new_re) + tuple(new_im)

    fin = lax.fori_loop(0, tc, step, init, unroll=8)
    for v in range(SUPER):
        hre_ref[0, v * SUBLANES:(v + 1) * SUBLANES, :] = fin[v]
        him_ref[0, v * SUBLANES:(v + 1) * SUBLANES, :] = fin[SUPER + v]

    for v in range(SUPER):
        cols = [sre[v, pl.ds(i, tc, stride=step_rows), :] for i in range(SUBLANES)]
        cols += [sim[v, pl.ds(i, tc, stride=step_rows), :] for i in range(SUBLANES)]
        hcat = jnp.concatenate(cols, axis=1).astype(BF16)
        y = jnp.dot(hcat, wc_ref[v], preferred_element_type=F32)
        y_ref[:, v * sgw:(v + 1) * sgw] = y + d_ref[:, v * sgw:(v + 1) * sgw] * u[:, v * sgw:(v + 1) * sgw]


def _s5_seq(u, wb, wc, a_re, a_im, d, h0re, h0im, l, nb, t, tc):
    s5w = u.shape[1]
    sgw = s5w // SUPER
    nk = t // tc
    rows = a_re.shape[1]
    st_spec = pl.BlockSpec((1, rows, LANES), lambda b, k: (b, 0, 0))
    return pl.pallas_call(
        functools.partial(_s5_seq_kernel, tc=tc, sgw=sgw),
        grid=(nb, nk),
        in_specs=[pl.BlockSpec((tc, s5w), lambda b, k: (b * nk + k, 0)),
                  _resident(wb, l), _resident(wc, l), _resident(a_re, l), _resident(a_im, l),
                  _resident(d, l), st_spec, st_spec],
        out_specs=[pl.BlockSpec((tc, s5w), lambda b, k: (b * nk + k, 0)), st_spec, st_spec],
        out_shape=[jax.ShapeDtypeStruct((nb * t, s5w), F32),
                   jax.ShapeDtypeStruct((nb, rows, LANES), F32), jax.ShapeDtypeStruct((nb, rows, LANES), F32)],
        scratch_shapes=[pltpu.VMEM((SUPER, tc * SUBLANES, LANES), F32),
                        pltpu.VMEM((SUPER, tc * SUBLANES, LANES), F32)],
        compiler_params=_params(("arbitrary", "arbitrary")),
        name="s5_seq",
    )(u, wb, wc, a_re, a_im, d, h0re, h0im)


def _s5_planes_kernel(u_ref, wb_ref, wc_ref, are_ref, aim_ref, d_ref, h0re_ref, h0im_ref,
                      y_ref, hre_ref, him_ref, *, nt, nseq, half):
    u = u_ref[...]
    r = jnp.dot(u.astype(BF16), wb_ref[0], preferred_element_type=F32)
    a_re, a_im = are_ref[...], aim_ref[...]
    hr, hi = h0re_ref[...], h0im_ref[...]
    hs_re, hs_im = [], []
    for t in range(nt):
        br = r[t * nseq:(t + 1) * nseq, :half]
        bi = r[t * nseq:(t + 1) * nseq, half:]
        hr, hi = a_re * hr - a_im * hi + br, a_re * hi + a_im * hr + bi
        hs_re.append(hr)
        hs_im.append(hi)
    hre_ref[...] = hr
    him_ref[...] = hi
    hcat = jnp.concatenate([jnp.concatenate(hs_re, axis=0), jnp.concatenate(hs_im, axis=0)], axis=1)
    y = jnp.dot(hcat.astype(BF16), wc_ref[0], preferred_element_type=F32)
    y_ref[...] = y + d_ref[...] * u


def _s5_planes(u, u_blk, wb, wc, a_re, a_im, d, h0re, h0im, l, nt, nseq):
    n, s5w = nt * nseq, u.shape[1]
    sgw = s5w // SUPER
    nstate = h0re.shape[2]
    half = nstate // SUPER
    return pl.pallas_call(
        functools.partial(_s5_planes_kernel, nt=nt, nseq=nseq, half=half),
        grid=(SUPER,),
        in_specs=[pl.BlockSpec((n, sgw), lambda v: (u_blk, v)),
                  pl.BlockSpec((None, 1) + wb.shape[2:], lambda v: (l, v, 0, 0)),
                  pl.BlockSpec((None, 1) + wc.shape[2:], lambda v: (l, v, 0, 0)),
                  pl.BlockSpec((None, 1, half), lambda v: (l, 0, v)),
                  pl.BlockSpec((None, 1, half), lambda v: (l, 0, v)),
                  pl.BlockSpec((None, 1, sgw), lambda v: (l, 0, v)),
                  pl.BlockSpec((None, nseq, half), lambda v: (l, 0, v)),
                  pl.BlockSpec((None, nseq, half), lambda v: (l, 0, v))],
        out_specs=[pl.BlockSpec((n, sgw), lambda v: (0, v)),
                   pl.BlockSpec((nseq, half), lambda v: (0, v)), pl.BlockSpec((nseq, half), lambda v: (0, v))],
        out_shape=[jax.ShapeDtypeStruct((n, s5w), F32),
                   jax.ShapeDtypeStruct((nseq, nstate), F32), jax.ShapeDtypeStruct((nseq, nstate), F32)],
        compiler_params=_params(("arbitrary",)),
        name="s5_planes",
    )(u, wb, wc, a_re, a_im, d, h0re, h0im)


CONV_ROW_TILE = 32
CONV_COL_TILE = 512


def _conv_seq_kernel(c_ref, halo_ref, w_ref, b_ref, g_ref, bt_ref, out_ref, xp_ref, sh_ref, y_ref,
                     *, tm, kk, halo, blocks_per_seq):
    i = pl.program_id(0)
    start = (i % blocks_per_seq) == 0
    ncol = c_ref.shape[1]
    xp_ref[0:halo, 0:ncol] = jnp.where(start, 0.0, halo_ref[...])
    xp_ref[halo:halo + tm, 0:ncol] = c_ref[...]
    ncopy = tm + halo - SUBLANES
    for b in range(1, SUBLANES):
        sh_ref[b - 1, :, 0:ncol] = xp_ref[pl.ds(b, ncopy), 0:ncol]
    rt = CONV_ROW_TILE
    ngrp = rt // SUBLANES
    for lc in range(ncol // CONV_COL_TILE):
        cols = slice(lc * CONV_COL_TILE, (lc + 1) * CONV_COL_TILE)

        def tile(r, carry, cols=cols):
            r0 = pl.multiple_of(r * rt, rt)
            acc = [jnp.broadcast_to(b_ref[:, cols], (SUBLANES, CONV_COL_TILE))] * ngrp
            for j in range(kk):
                a, b = divmod(halo - j, SUBLANES)
                k = kk - 1 - j
                w8 = w_ref[SUBLANES * k:SUBLANES * (k + 1), cols]
                for q in range(ngrp):
                    rows = pl.ds(r0 + SUBLANES * (a + q), SUBLANES)
                    xs = xp_ref[rows, cols] if b == 0 else sh_ref[b - 1, rows, cols]
                    acc[q] = acc[q] + w8 * xs
            for q in range(ngrp):
                y_ref[pl.ds(r0 + SUBLANES * q, SUBLANES), cols] = acc[q]
            return carry

        lax.fori_loop(0, tm // rt, tile, 0)
    out_ref[...] = jax.nn.silu(_ln(y_ref[...], g_ref[...], bt_ref[...]))


def _conv_seq(c, n, w, b, g, bt, l, t, tm, kk):
    cw = c.shape[1]
    halo = 32
    assert kk - 1 <= halo and tm % halo == 0 and t % tm == 0
    hb = tm // halo
    return pl.pallas_call(
        functools.partial(_conv_seq_kernel, tm=tm, kk=kk, halo=halo, blocks_per_seq=t // tm),
        grid=(n // tm,),
        in_specs=[pl.BlockSpec((tm, cw), lambda i: (i, 0)),
                  pl.BlockSpec((halo, cw), lambda i: (jnp.maximum(i * hb - 1, 0), 0)),
                  _resident(w, l), _resident(b, l), _resident(g, l), _resident(bt, l)],
        out_specs=pl.BlockSpec((tm, cw), lambda i: (i, 0)),
        out_shape=jax.ShapeDtypeStruct((n, cw), F32),
        scratch_shapes=[pltpu.VMEM((tm + halo, cw + LANES), F32),
                        pltpu.VMEM((SUBLANES - 1, tm + halo - SUBLANES, cw + LANES), F32),
                        pltpu.VMEM((tm, cw), F32)],
        compiler_params=_params(("arbitrary",)),
        name="conv_seq",
    )(c, c, w, b, g, bt)


def _conv_planes_kernel(cache_ref, c_ref, w_ref, b_ref, g_ref, bt_ref, out_ref, *, nt, kk):
    rows, cw = c_ref.shape[1], c_ref.shape[2]
    gam, bet = g_ref[...], bt_ref[...]
    for t in range(nt):
        acc = jnp.broadcast_to(b_ref[...], (rows, cw))
        for k in range(kk):
            p = t + k
            xs = cache_ref[p] if p < kk - 1 else c_ref[p - (kk - 1)]
            w8 = w_ref[SUBLANES * k:SUBLANES * (k + 1), :]
            acc = acc + jnp.concatenate([w8] * (rows // SUBLANES), axis=0) * xs
        out_ref[t] = jax.nn.silu(_ln(acc, gam, bet))


def _conv_planes(cache_t, c_t, w8, b, g, bt, l, kk, rows):
    nt, nseq, cw = c_t.shape
    return pl.pallas_call(
        functools.partial(_conv_planes_kernel, nt=nt, kk=kk),
        grid=(nseq // rows,),
        in_specs=[pl.BlockSpec((None, kk - 1, rows, cw), lambda i: (l, 0, i, 0)),
                  pl.BlockSpec((nt, rows, cw), lambda i: (0, i, 0)),
                  _resident(w8, l), _resident(b, l), _resident(g, l), _resident(bt, l)],
        out_specs=pl.BlockSpec((nt, rows, cw), lambda i: (0, i, 0)),
        out_shape=jax.ShapeDtypeStruct((nt, nseq, cw), F32),
        compiler_params=_params(("arbitrary",)),
        name="conv_planes",
    )(cache_t, c_t, w8, b, g, bt)


def _mix_out_kernel(y5a_ref, y5b_ref, caa_ref, cab_ref, xa_ref, xb_ref, wglu_ref, wout_ref, g_ref, b_ref, out_ref,
                    *, alpha, s5w, n_a):
    first = pl.program_id(0) < n_a
    rs = xa_ref.shape[0] // ROW_SPLIT
    for s in range(ROW_SPLIT):
        rows = slice(s * rs, (s + 1) * rs)
        ca = _pick_rows(first, caa_ref, cab_ref, rows)
        mix_c = jnp.dot(ca.astype(BF16), wout_ref[s5w:, :], preferred_element_type=F32)
        g5 = jax.nn.gelu(_pick_rows(first, y5a_ref, y5b_ref, rows))
        gate = jnp.dot(g5.astype(BF16), wglu_ref[...], preferred_element_type=F32)
        s5o = g5 * jax.nn.sigmoid(gate)
        mix = jnp.dot(s5o.astype(BF16), wout_ref[:s5w, :], preferred_element_type=F32) + mix_c
        x = _pick_rows(first, xa_ref, xb_ref, rows)
        out_ref[rows, :] = _ln(alpha * x + mix, g_ref[...], b_ref[...])


def _mix_out(y5a, y5b, caa, cab, xa, n_a, xb, b_blk, wglu, wout, g, b, l, alpha, tm):
    d = xa.shape[1]
    s5w, cw = y5a.shape[1], caa.shape[1]
    n = (n_a + 1) * tm
    return pl.pallas_call(
        functools.partial(_mix_out_kernel, alpha=alpha, s5w=s5w, n_a=n_a),
        grid=(n_a + 1,),
        in_specs=_two_sources(n_a, 0, tm, s5w) + _two_sources(n_a, 0, tm, cw) + _two_sources(n_a, b_blk, tm, d) +
        [_resident(wglu, l), _resident(wout, l), _resident(g, l), _resident(b, l)],
        out_specs=pl.BlockSpec((tm, d), lambda i: (i, 0)),
        out_shape=jax.ShapeDtypeStruct((n, d), F32),
        compiler_params=_params(("arbitrary",)),
        name="mix_out",
    )(y5a, y5b, caa, cab, xa, xb, wglu, wout, g, b)


FFN_HALO = 16


def _ffn_kernel(*refs, alpha, tm, planes, nseq, blocks_per_seq):
    if planes:
        (x_ref, hg_c_ref, hv_c_ref, wg_ref, wv_ref, wd_ref, cwg_ref, cwv_ref, cbg_ref, cbv_ref, g_ref, b_ref,
         out_ref, tg_ref, tv_ref, xb_ref) = refs
    else:
        (x_hbm, halo_ref, wg_ref, wv_ref, wd_ref, cwg_ref, cwv_ref, cbg_ref, cbv_ref, g_ref, b_ref,
         out_ref, tg_ref, tv_ref, xb_ref, x_ref, x_sem) = refs
    i = pl.program_id(0)
    j = pl.program_id(1)

    def x_copy(blk):
        return pltpu.make_async_copy(x_hbm.at[pl.ds(pl.multiple_of(blk * tm, tm), tm), :], x_ref, x_sem)

    @pl.when(j == 0)
    def _():
        if planes:
            xb_ref[...] = x_ref[...].astype(BF16)
            out_ref[...] = jnp.zeros_like(out_ref)
        else:
            @pl.when(i == 0)
            def _():
                x_copy(0).start()

            x_copy(i).wait()
            start = (i % blocks_per_seq) == 0
            xb_ref[0:FFN_HALO, :] = jnp.where(start, 0.0, halo_ref[...]).astype(BF16)
            xb_ref[FFN_HALO:, :] = x_ref[...].astype(BF16)
            out_ref[...] = alpha * x_ref[...]

    if not planes:
        @pl.when(jnp.logical_and(j == 1, i + 1 < pl.num_programs(0)))
        def _():
            x_copy(i + 1).start()

    xb = xb_ref[...]
    hg = jnp.dot(xb, wg_ref[...], preferred_element_type=F32)
    hv = jnp.dot(xb, wv_ref[...], preferred_element_type=F32)

    def conv3(h, hist_ref, cw_ref, cb_ref, tail_ref):
        w0, w1, w2 = cw_ref[0:1, :], cw_ref[1:2, :], cw_ref[2:3, :]
        if planes:
            hp = jnp.concatenate([hist_ref[...], h], axis=0)
            cur, prev1, prev2 = hp[2 * nseq:], hp[nseq:nseq + tm], hp[:tm]
            tail_ref[...] = h[tm - 2 * nseq:, :]
        else:
            cur = h[FFN_HALO:, :]
            prev1 = h[FFN_HALO - 1:FFN_HALO - 1 + tm, :]
            prev2 = h[FFN_HALO - 2:FFN_HALO - 2 + tm, :]
            tail_ref[...] = h[FFN_HALO + tm - SUBLANES:, :]
        return w2 * cur + w1 * prev1 + w0 * prev2 + cb_ref[...]

    cg = conv3(hg, hg_c_ref if planes else None, cwg_ref, cbg_ref, tg_ref)
    cv = conv3(hv, hv_c_ref if planes else None, cwv_ref, cbv_ref, tv_ref)
    act = (jax.nn.silu(cg) * cv).astype(BF16)
    out_ref[...] += jnp.dot(act, wd_ref[...], preferred_element_type=F32)

    @pl.when(j == pl.num_programs(1) - 1)
    def _():
        r = alpha * x_ref[...] + out_ref[...] if planes else out_ref[...]
        out_ref[...] = _ln(r, g_ref[...], b_ref[...])


def _ffn(x, n, w_up, w_down, conv_w, conv_b, g, b, l, alpha, tm, tf, *, cache=None, x_blk=0, t=None, nseq=None):
    d = x.shape[1]
    dff = w_down.shape[1]
    nj = dff // tf
    fk = conv_w.shape[1]
    planes = cache is not None
    wspecs = [pl.BlockSpec((None, d, tf), lambda i, j: (l, 0, j)),
              pl.BlockSpec((None, d, tf), lambda i, j: (l, 0, nj + j)),
              pl.BlockSpec((None, tf, d), lambda i, j: (l, j, 0)),
              pl.BlockSpec((None, fk, tf), lambda i, j: (l, 0, j)),
              pl.BlockSpec((None, fk, tf), lambda i, j: (l, 0, nj + j)),
              pl.BlockSpec((None, 1, tf), lambda i, j: (l, 0, j)),
              pl.BlockSpec((None, 1, tf), lambda i, j: (l, 0, nj + j)),
              pl.BlockSpec((None, 1, d), lambda i, j: (l, 0, 0)), pl.BlockSpec((None, 1, d), lambda i, j: (l, 0, 0))]
    wargs = [w_up, w_up, w_down, conv_w, conv_w, conv_b, conv_b, g, b]
    if planes:
        assert n == tm
        hist = cache.shape[1]
        in_specs = [pl.BlockSpec((tm, d), lambda i, j: (x_blk, 0)),
                    pl.BlockSpec((None, hist, tf), lambda i, j: (l, 0, j)),
                    pl.BlockSpec((None, hist, tf), lambda i, j: (l, 0, nj + j))] + wspecs
        args = [x, cache, cache] + wargs
        tail_rows, xb_rows, bps = hist, tm, 1
        scratch = [pltpu.VMEM((xb_rows, d), BF16)]
    else:
        hb = tm // FFN_HALO
        in_specs = [pl.BlockSpec(memory_space=pl.ANY),
                    pl.BlockSpec((FFN_HALO, d), lambda i, j: (jnp.maximum(i * hb - 1, 0), 0))] + wspecs
        args = [x, x] + wargs
        tail_rows, xb_rows, bps = SUBLANES, tm + FFN_HALO, t // tm
        scratch = [pltpu.VMEM((xb_rows, d), BF16), pltpu.VMEM((tm, d), F32), pltpu.SemaphoreType.DMA(())]
    nblk = n // tm
    tail_spec = pl.BlockSpec((tail_rows, tf), lambda i, j: (i, j))
    return pl.pallas_call(
        functools.partial(_ffn_kernel, alpha=alpha, tm=tm, planes=planes, nseq=nseq, blocks_per_seq=bps),
        grid=(nblk, nj),
        in_specs=in_specs,
        out_specs=[pl.BlockSpec((tm, d), lambda i, j: (i, 0)), tail_spec, tail_spec],
        out_shape=[jax.ShapeDtypeStruct((n, d), F32),
                   jax.ShapeDtypeStruct((nblk * tail_rows, dff), F32),
                   jax.ShapeDtypeStruct((nblk * tail_rows, dff), F32)],
        scratch_shapes=scratch,
        compiler_params=_params(("arbitrary", "arbitrary")),
        name="ffn_planes" if planes else "ffn_seq",
    )(*args)


def _pe_kernel(x_ref, p_ref, wg_ref, wp_ref, g_ref, b_ref, out_ref, *, alpha):
    rs = x_ref.shape[0] // ROW_SPLIT
    for s in range(ROW_SPLIT):
        rows = slice(s * rs, (s + 1) * rs)
        x = x_ref[rows, :]
        gate = jax.nn.sigmoid(jnp.dot(x.astype(BF16), wg_ref[...], preferred_element_type=F32))
        e = jnp.dot(p_ref[rows, :].astype(BF16), wp_ref[...], preferred_element_type=F32)
        out_ref[rows, :] = _ln(alpha * x + gate * e, g_ref[...], b_ref[...])


def _pe(x, p, wg, wp, g, b, l, alpha, tm):
    n, d = x.shape
    pd = p.shape[2]
    return pl.pallas_call(
        functools.partial(_pe_kernel, alpha=alpha),
        grid=(n // tm,),
        in_specs=[pl.BlockSpec((tm, d), lambda i: (i, 0)), pl.BlockSpec((None, tm, pd), lambda i: (l, i, 0)),
                  _resident(wg, l), _resident(wp, l), _resident(g, l), _resident(b, l)],
        out_specs=pl.BlockSpec((tm, d), lambda i: (i, 0)),
        out_shape=jax.ShapeDtypeStruct((n, d), F32),
        compiler_params=_params(("arbitrary",)),
        name="pe_embed",
    )(x, p, wg, wp, g, b)


def _pe2_kernel(xa_ref, xb_ref, pa_ref, pb_ref, wg_ref, wp_ref, g_ref, b_ref, out_ref, *, alpha, n_a):
    first = pl.program_id(0) < n_a
    rs = xa_ref.shape[0] // ROW_SPLIT
    for s in range(ROW_SPLIT):
        rows = slice(s * rs, (s + 1) * rs)
        x = _pick_rows(first, xa_ref, xb_ref, rows)
        gate = jax.nn.sigmoid(jnp.dot(x.astype(BF16), wg_ref[...], preferred_element_type=F32))
        e = jnp.dot(_pick_rows(first, pa_ref, pb_ref, rows).astype(BF16), wp_ref[...], preferred_element_type=F32)
        out_ref[rows, :] = _ln(alpha * x + gate * e, g_ref[...], b_ref[...])


def _pe2(xa, xb, pa, pb, wg, wp, g, b, l, alpha, tm):
    d = xa.shape[1]
    pd = pa.shape[2]
    n_a = xa.shape[0] // tm
    n = (n_a + 1) * tm
    return pl.pallas_call(
        functools.partial(_pe2_kernel, alpha=alpha, n_a=n_a),
        grid=(n_a + 1,),
        in_specs=_two_sources(n_a, 0, tm, d) +
        [pl.BlockSpec((None, tm, pd), lambda i: (l, jnp.minimum(i, n_a - 1), 0)),
         pl.BlockSpec((None, tm, pd), lambda i: (l, 0, 0)),
         _resident(wg, l), _resident(wp, l), _resident(g, l), _resident(b, l)],
        out_specs=pl.BlockSpec((tm, d), lambda i: (i, 0)),
        out_shape=jax.ShapeDtypeStruct((n, d), F32),
        compiler_params=_params(("arbitrary",)),
        name="pe_embed2",
    )(xa, xb, pa, pb, wg, wp, g, b)


def _s5_discretise(lam_re, lam_im, log_dt, b_re, b_im):
    lr, li = lam_re.astype(F32), lam_im.astype(F32)
    dt = jnp.exp(log_dt.astype(F32))[..., None]
    mag = jnp.exp(lr * dt)
    ab_re = mag * jnp.cos(li * dt)
    ab_im = mag * jnp.sin(li * dt)
    num_re, num_im = ab_re - 1.0, ab_im
    den = lr * lr + li * li
    q_re = (num_re * lr + num_im * li) / den
    q_im = (num_im * lr - num_re * li) / den
    br, bi = b_re.astype(F32), b_im.astype(F32)
    bb_re = q_re[..., None] * br - q_im[..., None] * bi
    bb_im = q_re[..., None] * bi + q_im[..., None] * br
    return ab_re, ab_im, bb_re, bb_im


def _blockdiag_kernel(tbr_ref, tbi_ref, tcr_ref, tci_ref, eb_ref, ec_ref, wb_ref, wc_ref, *, gs, h, p):
    def place(t_ref, e_ref, rows_per_block, cols_per_block):
        tiled = jnp.dot(t_ref[...].astype(BF16), e_ref[...], preferred_element_type=F32)
        rg = lax.broadcasted_iota(jnp.int32, tiled.shape, 0) // rows_per_block
        cg = lax.broadcasted_iota(jnp.int32, tiled.shape, 1) // cols_per_block
        return jnp.where(rg == cg, tiled, 0.0).astype(BF16)

    wb_ref[:, :gs * p] = place(tbr_ref, eb_ref, h, p)
    wb_ref[:, gs * p:] = place(tbi_ref, eb_ref, h, p)
    wc_ref[:gs * p, :] = place(tcr_ref, ec_ref, p, h)
    wc_ref[gs * p:, :] = place(tci_ref, ec_ref, p, h)


def _s5_block_weights(bb_re, bb_im, c_re, c_im):
    nl, g, p, h = bb_re.shape
    gs = g // SUPER
    n = nl * SUPER
    tb = [v.transpose(0, 1, 3, 2).reshape(n, gs * h, p) for v in (bb_re, bb_im)]
    tc = [v.astype(F32).transpose(0, 1, 3, 2).reshape(n, gs * p, h) for v in (c_re, -c_im)]
    eb = jnp.tile(jnp.eye(p, dtype=BF16), (1, gs))
    ec = jnp.tile(jnp.eye(h, dtype=BF16), (1, gs))
    blk = lambda a: pl.BlockSpec((None,) + a.shape[1:], lambda i: (i, 0, 0))
    whole = lambda a: pl.BlockSpec(a.shape, lambda i: (0, 0))
    wb, wc = pl.pallas_call(
        functools.partial(_blockdiag_kernel, gs=gs, h=h, p=p),
        grid=(n,),
        in_specs=[blk(tb[0]), blk(tb[1]), blk(tc[0]), blk(tc[1]), whole(eb), whole(ec)],
        out_specs=[pl.BlockSpec((None, gs * h, 2 * gs * p), lambda i: (i, 0, 0)),
                   pl.BlockSpec((None, 2 * gs * p, gs * h), lambda i: (i, 0, 0))],
        out_shape=[jax.ShapeDtypeStruct((n, gs * h, 2 * gs * p), BF16),
                   jax.ShapeDtypeStruct((n, 2 * gs * p, gs * h), BF16)],
        compiler_params=_params(("arbitrary",)),
        name="s5_blockdiag",
    )(tb[0], tb[1], tc[0], tc[1], eb, ec)
    return wb.reshape(nl, SUPER, gs * h, 2 * gs * p), wc.reshape(nl, SUPER, 2 * gs * p, gs * h)


def _rows(v):
    return v.astype(F32).reshape(v.shape[0], 1, -1)


def kernel(x_prompt, x_sample, state_s5_re, state_s5_im, cache_conv, cache_ffn_conv, p_prompt, p_sample,
           w_in, s5_lam_re, s5_lam_im, s5_log_dt, s5_b_re, s5_b_im, s5_c_re, s5_c_im, s5_d, s5_w_glu,
           conv_w, conv_b, conv_ln_g, conv_ln_b, w_out, ln1_g, ln1_b,
           ffn_w_up, ffn_conv_w, ffn_conv_b, ffn_w_down, ln2_g, ln2_b,
           pe_w, pe_w_gate, ln3_g, ln3_b):
    nb, t, d = x_prompt.shape
    ns, ts, _ = x_sample.shape
    depth = w_in.shape[0]
    groups, nstate_g = s5_lam_re.shape[1], s5_lam_re.shape[2]
    s5w = groups * s5_d.shape[2]
    cw = d - s5w
    nstate = groups * nstate_g
    kk = conv_w.shape[1]
    fk = ffn_conv_w.shape[1]
    dff = ffn_w_down.shape[1]
    assert fk == 3 and ts >= fk - 1 and ts <= kk - 1 and t >= kk - 1
    alpha = (2.0 * depth) ** 0.25
    tm, tm_ffn, tf, tc, tconv = 512, 1024, 512, 512, 512

    xp = x_prompt.reshape(nb * t, d)
    xs = x_sample.transpose(1, 0, 2).reshape(ts * ns, d)
    zero_state = jnp.zeros((nb, nstate // LANES, LANES), F32)

    w_in_bf, wglu_bf, wout_bf = w_in.astype(BF16), s5_w_glu.astype(BF16), w_out.astype(BF16)
    wup_bf, wdown_bf = ffn_w_up.astype(BF16), ffn_w_down.astype(BF16)
    wpe_bf, wgate_bf = pe_w.astype(BF16), pe_w_gate.astype(BF16)
    ab_re, ab_im, bb_re, bb_im = _s5_discretise(s5_lam_re, s5_lam_im, s5_log_dt, s5_b_re, s5_b_im)
    wb, wc = _s5_block_weights(bb_re, bb_im, s5_c_re, s5_c_im)
    a_tile_re, a_tile_im = (v.reshape(depth, nstate // LANES, LANES) for v in (ab_re, ab_im))
    a_row_re, a_row_im = (v.reshape(depth, 1, nstate) for v in (ab_re, ab_im))
    d_rows = _rows(s5_d.reshape(depth, s5w))
    conv_w1 = conv_w.astype(F32)
    conv_w8 = jnp.repeat(conv_w1, SUBLANES, axis=1)
    conv_b_rows, cg_rows, cb_rows = _rows(conv_b), _rows(conv_ln_g), _rows(conv_ln_b)
    fcw, fcb = ffn_conv_w.astype(F32), _rows(ffn_conv_b)
    l1g, l1b, l2g, l2b, l3g, l3b = (_rows(v) for v in (ln1_g, ln1_b, ln2_g, ln2_b, ln3_g, ln3_b))
    pp = p_prompt.reshape(depth, nb * t, -1)
    ps = p_sample.transpose(0, 2, 1, 3).reshape(depth, ts * ns, -1)
    h0_re = state_s5_re.reshape(depth, ns, nstate).astype(F32)
    h0_im = state_s5_im.reshape(depth, ns, nstate).astype(F32)
    cache_t = cache_conv.transpose(0, 2, 1, 3)
    fcache = cache_ffn_conv.transpose(0, 2, 1, 3).reshape(depth, (fk - 1) * ns, 2 * dff)

    p_re, p_im, p_cv, p_tg, p_tv = [], [], [], [], []
    s_re, s_im, s_c, s_tg, s_tv = [], [], [], [], []
    n_p, n_s = nb * t, ts * ns
    assert n_s == tm and n_p % tm == 0
    nblk_p = n_p // tm
    x_seq, x_pl, x_pl_blk = xp, xs, 0
    for l in range(depth):
        u, c = _in_proj(x_seq, nblk_p, x_pl, x_pl_blk, w_in_bf, l, s5w, cw, tm)
        c_pl = c[n_p:]
        y5p, hre_p, him_p = _s5_seq(u, wb, wc, a_tile_re, a_tile_im, d_rows, zero_state, zero_state, l, nb, t, tc)
        y5s, hre_s, him_s = _s5_planes(u, nblk_p, wb, wc, a_row_re, a_row_im, d_rows, h0_re, h0_im, l, ts, ns)
        cap = _conv_seq(c, n_p, conv_w8, conv_b_rows, cg_rows, cb_rows, l, t, tconv, kk)
        cas = _conv_planes(cache_t, c_pl.reshape(ts, ns, cw), conv_w8, conv_b_rows, cg_rows, cb_rows, l, kk, 32)
        x1 = _mix_out(y5p, y5s, cap, cas.reshape(n_s, cw), x_seq, nblk_p, x_pl, x_pl_blk,
                      wglu_bf, wout_bf, l1g, l1b, l, alpha, tm)
        x2p, tg_p, tv_p = _ffn(x1, n_p, wup_bf, wdown_bf, fcw, fcb, l2g, l2b, l, alpha, tm_ffn, tf, t=t)
        x2s, tg_s, tv_s = _ffn(x1, n_s, wup_bf, wdown_bf, fcw, fcb, l2g, l2b, l, alpha, n_s, tf,
                               cache=fcache, x_blk=nblk_p, nseq=ns)
        if l + 1 < depth:
            x_all = _pe2(x2p, x2s, pp, ps, wgate_bf, wpe_bf, l3g, l3b, l, alpha, tm)
            x_seq, x_pl, x_pl_blk = x_all, x_all, nblk_p
        else:
            xp = _pe(x2p, pp, wgate_bf, wpe_bf, l3g, l3b, l, alpha, tm)
            xs = _pe(x2s, ps, wgate_bf, wpe_bf, l3g, l3b, l, alpha, tm)
        p_re.append(hre_p)
        p_im.append(him_p)
        p_cv.append(jnp.stack([c[(b + 1) * t - (kk - 1):(b + 1) * t] for b in range(nb)]))
        p_tg.append(tg_p)
        p_tv.append(tv_p)
        s_re.append(hre_s)
        s_im.append(him_s)
        s_c.append(c_pl)
        s_tg.append(tg_s)
        s_tv.append(tv_s)

    y_prompt = xp.reshape(nb, t, d)
    y_sample = xs.reshape(ts, ns, d).transpose(1, 0, 2)
    bps = t // tm_ffn
    p_tails = jnp.concatenate([jnp.stack(p_tg), jnp.stack(p_tv)], axis=2).reshape(depth, nb, bps, SUBLANES, 2 * dff)
    p_ff = p_tails[:, :, bps - 1, SUBLANES - (fk - 1):]
    s_cv = jnp.concatenate([cache_conv[:, :, ts:], jnp.stack(s_c).reshape(depth, ts, ns, cw).transpose(0, 2, 1, 3)],
                           axis=2)
    s_ff = jnp.concatenate([jnp.stack(s_tg), jnp.stack(s_tv)], axis=2)
    s_ff = s_ff.reshape(depth, fk - 1, ns, 2 * dff).transpose(0, 2, 1, 3)
    return (y_prompt, y_sample,
            jnp.stack(p_re).reshape(depth, nb, groups, nstate_g), jnp.stack(p_im).reshape(depth, nb, groups, nstate_g),
            jnp.stack(p_cv), p_ff,
            jnp.stack(s_re).reshape(depth, ns, groups, nstate_g), jnp.stack(s_im).reshape(depth, ns, groups, nstate_g),
            s_cv, s_ff)
```
